```python
import jax, jax.numpy as jnp
from jax import lax
import numpy as np

D_MODEL = 2048
BATCH = 16
SEQ = 256
DEPTH = 4
DEC_BATCH = 8
DEC_SEQ = 2048
PAST_LEN = 256

GRID_W = 64
HEAD_DIM = 128
POOL_WINDOWS = (2, 4, 8, 16)
POOL_GROUP = D_MODEL // 16
POOL_WIDTH = 4 * POOL_GROUP
NA_HEADS = D_MODEL // 4 // HEAD_DIM
NA_WIDTH = NA_HEADS * HEAD_DIM
GQA_HEADS = D_MODEL // 2 // HEAD_DIM
GQA_KV_HEADS = GQA_HEADS // 4
GQA_WIDTH = GQA_HEADS * HEAD_DIM
GQA_KV_WIDTH = GQA_KV_HEADS * HEAD_DIM
MIX_WIDTH = POOL_WIDTH + NA_WIDTH + GQA_WIDTH
IN_SIZES = (POOL_WIDTH, NA_WIDTH, NA_WIDTH, NA_WIDTH, GQA_WIDTH, GQA_KV_WIDTH, GQA_KV_WIDTH)
IN_WIDTH = POOL_WIDTH + 3 * NA_WIDTH + GQA_WIDTH + 2 * GQA_KV_WIDTH
NA_ROWS = 8
NA_COLS = 16
N_EXPERTS = 32
TOP_K = 4
D_FF = D_MODEL // 4
MOE_BLOCK = 32
SWIGLU_LIMIT = 7.0
SWIGLU_ALPHA = 1.702
ROPE_THETA = 10000.0
Q_BLOCK = 128
EPS = 1e-6

kernel_name = 'hybrid_pool_natten_gqa_moe_diffusion_step'


def rmsnorm(x, g):
    xf = x.astype(jnp.float32)
    y = xf * lax.rsqrt(jnp.mean(xf * xf, axis=-1, keepdims=True) + EPS)
    return (y * g.astype(jnp.float32)).astype(x.dtype)


def axial_rope_tables(T):
    t = jnp.arange(T)
    row = (t // GRID_W).astype(jnp.float32)
    col = (t % GRID_W).astype(jnp.float32)
    half = HEAD_DIM // 2
    inv = ROPE_THETA ** (-jnp.arange(0, half, 2, dtype=jnp.float32) / half)
    ang = jnp.concatenate([row[:, None] * inv, col[:, None] * inv], axis=-1)
    return jnp.cos(ang), jnp.sin(ang)


def apply_rope(x, cos, sin):
    xf = x.astype(jnp.float32)
    half = HEAD_DIM // 2
    x1, x2 = xf[..., :half], xf[..., half:]
    c = cos[None, :, None, :]
    s = sin[None, :, None, :]
    return jnp.concatenate([x1 * c - x2 * s, x1 * s + x2 * c], axis=-1).astype(x.dtype)


def pool_mixer(u, w_pool, pool_scale):
    B, L, _ = u.shape
    uf = u.astype(jnp.float32)
    csum = jnp.concatenate([jnp.zeros((B, 1, POOL_WIDTH), jnp.float32), jnp.cumsum(uf, axis=1)], axis=1)
    t = jnp.arange(L)
    outs = []
    for gi, w in enumerate(POOL_WINDOWS):
        seg = csum[..., gi * POOL_GROUP:(gi + 1) * POOL_GROUP]
        lo = jnp.clip(t - w // 2, 0, L)
        hi = jnp.clip(t + w // 2, 0, L)
        cnt = (hi - lo).astype(jnp.float32)
        outs.append((seg[:, hi] - seg[:, lo]) / cnt[None, :, None])
    pooled = jnp.concatenate(outs, axis=-1).astype(u.dtype) - u
    pooled = pooled.reshape(B, L, len(POOL_WINDOWS), POOL_GROUP)
    y = jnp.einsum('blgc,gcd->blgd', pooled, w_pool).reshape(B, L, POOL_WIDTH)
    return y * pool_scale


def block_attention(q, k, v):
    B, Lq, Hq, Dh = q.shape
    Hkv = k.shape[2]
    G = Hq // Hkv
    nb = Lq // Q_BLOCK
    scale = Dh ** -0.5
    qb = q.reshape(B, nb, Q_BLOCK, Hkv, G, Dh).transpose(1, 0, 2, 3, 4, 5)

    def one_block(qblk):
        s = jnp.einsum('bqkgd,bskd->bkgqs', qblk, k, preferred_element_type=jnp.float32) * scale
        p = jax.nn.softmax(s, axis=-1).astype(v.dtype)
        return jnp.einsum('bkgqs,bskd->bqkgd', p, v)

    out = lax.map(one_block, qb)
    return out.transpose(1, 0, 2, 3, 4, 5).reshape(B, Lq, Hq * Dh)


def neighbourhood_attention(q, k, v, ck, cv, rel_bias):
    B, T, H, Dh = q.shape
    rows = T // GRID_W
    wr = min(NA_ROWS, rows)
    scale = Dh ** -0.5
    i = jnp.arange(rows)
    rs = jnp.clip(i - wr // 2, 0, rows - wr)
    band_rows = rs[:, None] + jnp.arange(wr)[None, :]
    kg = k.reshape(B, rows, GRID_W, H, Dh)[:, band_rows].reshape(B, rows, wr * GRID_W, H, Dh)
    vg = v.reshape(B, rows, GRID_W, H, Dh)[:, band_rows].reshape(B, rows, wr * GRID_W, H, Dh)
    qg = q.reshape(B, rows, GRID_W, H, Dh)
    s_lat = jnp.einsum('brqhd,brkhd->bhrqk', qg, kg, preferred_element_type=jnp.float32) * scale
    j = jnp.arange(GRID_W)
    cs = jnp.clip(j - NA_COLS // 2, 0, GRID_W - NA_COLS)
    col_ok = (j[None, :] >= cs[:, None]) & (j[None, :] < cs[:, None] + NA_COLS)
    dr_idx = band_rows - i[:, None] + (NA_ROWS - 1)
    dc_idx = jnp.clip(j[None, :] - j[:, None] + (NA_COLS - 1), 0, 2 * NA_COLS - 2)
    bias = rel_bias[:, dr_idx[:, None, :, None], dc_idx[None, :, None, :]]
    bias = bias.reshape(H, rows, GRID_W, wr * GRID_W).astype(jnp.float32)
    mask = jnp.broadcast_to(col_ok[:, None, :], (GRID_W, wr, GRID_W)).reshape(GRID_W, wr * GRID_W)
    s_lat = jnp.where(mask, s_lat + bias[None], -jnp.inf)
    s_ctx = jnp.einsum('brqhd,bshd->bhrqs', qg, ck, preferred_element_type=jnp.float32) * scale
    p = jax.nn.softmax(jnp.concatenate([s_lat, s_ctx], axis=-1), axis=-1).astype(v.dtype)
    nk = wr * GRID_W
    out = (jnp.einsum('bhrqk,brkhd->brqhd', p[..., :nk], vg)
           + jnp.einsum('bhrqs,bshd->brqhd', p[..., nk:], cv))
    return out.reshape(B, T, H * Dh)


def moe_ffn(h, w_router, b_router, w_gate_up, b_gate_up, w_down, b_down):
    B, L, D = h.shape
    N = B * L
    M = N * TOP_K
    xt = h.reshape(N, D)
    logits = (xt @ w_router + b_router).astype(jnp.float32)
    top_val, top_idx = lax.top_k(logits, TOP_K)
    gates = jax.nn.softmax(top_val, axis=-1).astype(h.dtype)
    flat_e = top_idx.reshape(-1)
    order = jnp.argsort(flat_e)
    sorted_e = flat_e[order]
    sizes = jnp.bincount(flat_e, length=N_EXPERTS)
    padded = (sizes + MOE_BLOCK - 1) // MOE_BLOCK * MOE_BLOCK
    start = jnp.cumsum(sizes) - sizes
    pend = jnp.cumsum(padded)
    pstart = pend - padded
    dest = pstart[sorted_e] + jnp.arange(M) - start[sorted_e]
    n_blocks = -(-M // MOE_BLOCK) + N_EXPERTS
    xs = jnp.zeros((n_blocks * MOE_BLOCK, D), h.dtype).at[dest].set(xt[order // TOP_K])
    blk_start = jnp.arange(n_blocks) * MOE_BLOCK
    blk_e = jnp.clip(jnp.searchsorted(pend, blk_start, side='right'), 0, N_EXPERTS - 1)
    blk_used = blk_start < pend[-1]

    def expert_block(args):
        xb, e, used = args

        def run(xb):
            gu = xb @ w_gate_up[e] + b_gate_up[e]
            gate = jnp.minimum(gu[:, :D_FF], SWIGLU_LIMIT)
            up = jnp.clip(gu[:, D_FF:], -SWIGLU_LIMIT, SWIGLU_LIMIT)
            act = (up + 1) * (gate * jax.nn.sigmoid(SWIGLU_ALPHA * gate))
            return (act @ w_down[e] + b_down[e]).astype(xb.dtype)

        return lax.cond(used, run, jnp.zeros_like, xb)

    yb = lax.map(expert_block, (xs.reshape(n_blocks, MOE_BLOCK, D), blk_e, blk_used))
    y_sorted = yb.reshape(n_blocks * MOE_BLOCK, D)[dest]
    y_rows = jnp.zeros((M, D), h.dtype).at[order].set(y_sorted)
    y = jnp.einsum('nkd,nk->nd', y_rows.reshape(N, TOP_K, D), gates)
    return y.reshape(B, L, D)


def mixer_inputs(x, mod, g1, w_in, g_q, g_k):
    D = D_MODEL
    h = rmsnorm(x, g1) * (1 + mod[..., D:2 * D]) + mod[..., :D]
    B, L, _ = x.shape
    splits = np.cumsum(IN_SIZES)[:-1].tolist()
    u, qn, kn, vn, qg, kg, vg = jnp.split(h @ w_in, splits, axis=-1)
    heads = lambda t: t.reshape(B, L, -1, HEAD_DIM)
    return (u, heads(qn), heads(kn), heads(vn),
            rmsnorm(heads(qg), g_q), rmsnorm(heads(kg), g_k), heads(vg))


def finish_layer(x, mod, ya, yb, yc, w_out, g2, moe_p):
    D = D_MODEL
    x = x + mod[..., 2 * D:3 * D] * (jnp.concatenate([ya, yb, yc], axis=-1) @ w_out)
    h = rmsnorm(x, g2) * (1 + mod[..., 4 * D:5 * D]) + mod[..., 3 * D:4 * D]
    return x + mod[..., 5 * D:] * moe_ffn(h, *moe_p)


def context_layer(x, mod, g1, g2, w_in, w_pool, pool_scale, g_q, g_k, w_out, moe_p):
    u, qn, kn, vn, qg, kg, vg = mixer_inputs(x, mod, g1, w_in, g_q, g_k)
    ya = pool_mixer(u, w_pool, pool_scale)
    yb = block_attention(qn, kn, vn)
    yc = block_attention(qg, kg, vg)
    return finish_layer(x, mod, ya, yb, yc, w_out, g2, moe_p), kn, vn, kg, vg


def latent_layer(x, mod, ck_na, cv_na, ck_g, cv_g, cos, sin, rel_bias,
                 g1, g2, w_in, w_pool, pool_scale, g_q, g_k, w_out, moe_p):
    u, qn, kn, vn, qg, kg, vg = mixer_inputs(x, mod, g1, w_in, g_q, g_k)
    ya = pool_mixer(u, w_pool, pool_scale)
    yb = neighbourhood_attention(qn, kn, vn, ck_na, cv_na, rel_bias)
    qg = apply_rope(qg, cos, sin)
    kg = apply_rope(kg, cos, sin)
    yc = block_attention(qg, jnp.concatenate([kg, ck_g], axis=1), jnp.concatenate([vg, cv_g], axis=1))
    return finish_layer(x, mod, ya, yb, yc, w_out, g2, moe_p)


def setup_inputs(seed: int = 0) -> dict:
    key = jax.random.key(seed)
    ks = jax.random.split(key, 28)
    D = D_MODEL
    nrm = lambda k, shape, s: jax.random.normal(k, shape, jnp.float32) * s
    return {
        'x_prompt': nrm(ks[0], (BATCH, SEQ, D), 1.0),
        'x_sample': nrm(ks[1], (DEC_BATCH, DEC_SEQ, D), 1.0),
        'cache_na_k': nrm(ks[2], (DEC_BATCH, DEPTH, PAST_LEN, NA_HEADS, HEAD_DIM), 1.0),
        'cache_na_v': nrm(ks[3], (DEC_BATCH, DEPTH, PAST_LEN, NA_HEADS, HEAD_DIM), 1.0),
        'cache_gqa_k': nrm(ks[4], (DEC_BATCH, DEPTH, PAST_LEN, GQA_KV_HEADS, HEAD_DIM), 1.0),
        'cache_gqa_v': nrm(ks[5], (DEC_BATCH, DEPTH, PAST_LEN, GQA_KV_HEADS, HEAD_DIM), 1.0),
        'c': nrm(ks[6], (DEC_BATCH, D), 1.0),
        'c_ctx': nrm(ks[7], (D,), 1.0),
        'w_ada': nrm(ks[8], (DEPTH, D, 6 * D), 0.5 * D ** -0.5),
        'b_ada': nrm(ks[9], (DEPTH, 6 * D), 0.02),
        'g_norm1': 1.0 + nrm(ks[10], (DEPTH, D), 0.1),
        'g_norm2': 1.0 + nrm(ks[11], (DEPTH, D), 0.1),
        'w_in': nrm(ks[12], (DEPTH, D, IN_WIDTH), D ** -0.5),
        'w_pool': nrm(ks[13], (DEPTH, len(POOL_WINDOWS), POOL_GROUP, POOL_GROUP), POOL_GROUP ** -0.5),
        'pool_scale': 1.0 + nrm(ks[14], (DEPTH, POOL_WIDTH), 0.1),
        'na_rel_bias': nrm(ks[15], (DEPTH, NA_HEADS, 2 * NA_ROWS - 1, 2 * NA_COLS - 1), 0.1),
        'g_q': 1.0 + nrm(ks[16], (DEPTH, HEAD_DIM), 0.1),
        'g_k': 1.0 + nrm(ks[17], (DEPTH, HEAD_DIM), 0.1),
        'w_out': nrm(ks[18], (DEPTH, MIX_WIDTH, D), MIX_WIDTH ** -0.5),
        'w_router': nrm(ks[19], (DEPTH, D, N_EXPERTS), D ** -0.5),
        'b_router': nrm(ks[20], (DEPTH, N_EXPERTS), 0.01),
        'w_gate_up': nrm(ks[21], (DEPTH, N_EXPERTS, D, 2 * D_FF), D ** -0.5),
        'b_gate_up': nrm(ks[22], (DEPTH, N_EXPERTS, 2 * D_FF), 0.01),
        'w_down': nrm(ks[23], (DEPTH, N_EXPERTS, D_FF, D), D_FF ** -0.5),
        'b_down': nrm(ks[24], (DEPTH, N_EXPERTS, D), 0.01),
        'g_final': 1.0 + nrm(ks[25], (D,), 0.1),
    }


def reference(x_prompt, x_sample, cache_na_k, cache_na_v, cache_gqa_k, cache_gqa_v, c, c_ctx,
              w_ada, b_ada, g_norm1, g_norm2, w_in, w_pool, pool_scale, na_rel_bias, g_q, g_k,
              w_out, w_router, b_router, w_gate_up, b_gate_up, w_down, b_down, g_final):
    xc = x_prompt
    xl = x_sample
    cos, sin = axial_rope_tables(xl.shape[1])
    na_k, na_v, gq_k, gq_v = [], [], [], []
    for l in range(DEPTH):
        moe_p = (w_router[l], b_router[l], w_gate_up[l], b_gate_up[l], w_down[l], b_down[l])
        mod_ctx = (jax.nn.silu(c_ctx) @ w_ada[l] + b_ada[l])[None, None, :]
        mod_lat = (jax.nn.silu(c) @ w_ada[l] + b_ada[l])[:, None, :]
        xc, kn, vn, kg, vg = context_layer(xc, mod_ctx, g_norm1[l], g_norm2[l], w_in[l], w_pool[l],
                                           pool_scale[l], g_q[l], g_k[l], w_out[l], moe_p)
        na_k.append(kn)
        na_v.append(vn)
        gq_k.append(kg)
        gq_v.append(vg)
        xl = latent_layer(xl, mod_lat, cache_na_k[:, l], cache_na_v[:, l], cache_gqa_k[:, l],
                          cache_gqa_v[:, l], cos, sin, na_rel_bias[l], g_norm1[l], g_norm2[l],
                          w_in[l], w_pool[l], pool_scale[l], g_q[l], g_k[l], w_out[l], moe_p)
    y_prompt = rmsnorm(xc, g_final)
    y_sample = rmsnorm(xl, g_final)
    new_na_k = jnp.stack(na_k, axis=1)
    new_na_v = jnp.stack(na_v, axis=1)
    new_gqa_k = jnp.stack(gq_k, axis=1)
    new_gqa_v = jnp.stack(gq_v, axis=1)
    return (y_prompt, y_sample, new_na_k, new_na_v, new_gqa_k, new_gqa_v)
```

```python
import functools

import numpy as np
import jax
import jax.numpy as jnp
from jax import lax
from jax.experimental import pallas as pl
from jax.experimental.pallas import tpu as pltpu

F32 = jnp.float32
BF16 = jnp.bfloat16
I32 = jnp.int32

D_MODEL = 2048
HEAD_DIM = 128
CTX_SEQ = 256
LAT_SEQ = 2048
GRID_W = 64
GRID_ROWS = LAT_SEQ // GRID_W
POOL_WINDOWS = (2, 4, 8, 16)
POOL_GROUP = 128
POOL_WIDTH = 512
NA_HEADS = 4
NA_WIDTH = 512
NA_ROWS = 8
NA_COLS = 16
GQA_HEADS = 8
GQA_KV_HEADS = 2
GQA_GROUP = GQA_HEADS // GQA_KV_HEADS
GQA_WIDTH = 1024
GQA_KV_WIDTH = 256
IN_WIDTH = 3584
N_EXPERTS = 32
TOP_K = 4
D_FF = 512
SWIGLU_LIMIT = 7.0
SWIGLU_ALPHA = 1.702
ROPE_THETA = 10000.0
EPS = 1e-6
NEG = -1e30

COL_U, COL_QN, COL_KN, COL_VN, COL_QG, COL_KG, COL_VG = 0, 512, 1024, 1536, 2048, 3072, 3328

LANES = 128
COND_ROWS = 16
ATT_TILE = 256
NA_TILE_ROWS = ATT_TILE // GRID_W
NA_WIN_ROWS = NA_ROWS + NA_TILE_ROWS - 1
NA_WIN = NA_WIN_ROWS * GRID_W
INPROJ_TM = 1024
INPROJ_TN = 512
OUTPROJ_TM = 512
ROUTE_TM = 512
MOE_BLOCK = 512
TOKEN_TILE = 256
VMEM_LIMIT = 56 * 1024 * 1024


def _cparams(*sem):
    return pltpu.CompilerParams(dimension_semantics=sem, vmem_limit_bytes=VMEM_LIMIT)


def _cond_row(tile, tile_rows, n_ctx):
    row0 = tile * tile_rows
    return jnp.where(row0 < n_ctx, 0, 1 + (row0 - n_ctx) // LAT_SEQ)


def _ada_kernel(c_ref, w_ref, b_ref, o_ref):
    c = c_ref[...]
    s = (c * jax.nn.sigmoid(c)).astype(BF16)
    o_ref[0, 0] = jnp.dot(s, w_ref[0].astype(BF16), preferred_element_type=F32) + b_ref[0]


def _ada_mod(cond, w_ada, b_ada):
    depth = w_ada.shape[0]
    tn = 1024
    per = D_MODEL // tn
    return pl.pallas_call(
        _ada_kernel,
        out_shape=jax.ShapeDtypeStruct((depth, 6, COND_ROWS, D_MODEL), F32),
        grid=(depth, 6 * per),
        in_specs=[
            pl.BlockSpec((COND_ROWS, D_MODEL), lambda l, j: (0, 0)),
            pl.BlockSpec((1, D_MODEL, tn), lambda l, j: (l, 0, j)),
            pl.BlockSpec((1, 1, tn), lambda l, j: (l, 0, j)),
        ],
        out_specs=pl.BlockSpec((1, 1, COND_ROWS, tn), lambda l, j: (l, j // per, 0, j % per)),
        compiler_params=_cparams("arbitrary", "arbitrary"),
        name="ada_mod",
    )(cond, w_ada, b_ada.reshape(depth, 1, 6 * D_MODEL))


def _head_rms(x, g):
    return x * lax.rsqrt(jnp.mean(x * x, axis=-1, keepdims=True) + EPS) * g


def _inproj_kernel(x_ref, shift_ref, scale_ref, g1_ref, w_ref, gq_ref, gk_ref, cos_ref, sin_ref,
                   of_ref, ob_ref, h_scr, *, n_ctx):
    i = pl.program_id(0)
    j = pl.program_id(1)
    crow = _cond_row(i, INPROJ_TM, n_ctx)
    is_lat = i * INPROJ_TM >= n_ctx

    @pl.when(j == 0)
    def _():
        x = x_ref[...]
        y = x * lax.rsqrt(jnp.mean(x * x, axis=-1, keepdims=True) + EPS) * g1_ref[...]
        shift = shift_ref[0, 0, pl.ds(crow, 1), :]
        scale = scale_ref[0, 0, pl.ds(crow, 1), :]
        h_scr[...] = (y * (1 + scale) + shift).astype(BF16)

    acc = jnp.dot(h_scr[...], w_ref[...], preferred_element_type=F32)

    def rope(xh):
        cos = jnp.where(is_lat, cos_ref[...], 1.0)
        sin = jnp.where(is_lat, sin_ref[...], 0.0)
        return xh * cos + pltpu.roll(xh, HEAD_DIM // 2, 1) * sin

    def store(vals):
        for h, v in enumerate(vals):
            of_ref[:, h * HEAD_DIM:(h + 1) * HEAD_DIM] = v
            ob_ref[:, h * HEAD_DIM:(h + 1) * HEAD_DIM] = v.astype(BF16)

    heads = [acc[:, h * HEAD_DIM:(h + 1) * HEAD_DIM] for h in range(INPROJ_TN // HEAD_DIM)]
    jq0 = COL_QG // INPROJ_TN
    jk = COL_KG // INPROJ_TN

    @pl.when(j < jq0)
    def _():
        store(heads)

    @pl.when((j >= jq0) & (j < jk))
    def _():
        store([rope(_head_rms(xh, gq_ref[...])) for xh in heads])

    @pl.when(j == jk)
    def _():
        nk = GQA_KV_WIDTH // HEAD_DIM
        store([rope(_head_rms(xh, gk_ref[...])) for xh in heads[:nk]] + heads[nk:])


def _inproj(x, mod, layer, g1, w_in_b, g_q, g_k, cos2, sin2, n_ctx):
    n_tok = x.shape[0]
    lat_tiles = LAT_SEQ // INPROJ_TM
    ctx_tiles = n_ctx // INPROJ_TM

    def rope_map(i, j):
        return (jnp.maximum(i - ctx_tiles, 0) % lat_tiles, 0)

    return pl.pallas_call(
        functools.partial(_inproj_kernel, n_ctx=n_ctx),
        out_shape=(jax.ShapeDtypeStruct((n_tok, IN_WIDTH), F32),
                   jax.ShapeDtypeStruct((n_tok, IN_WIDTH), BF16)),
        grid=(n_tok // INPROJ_TM, IN_WIDTH // INPROJ_TN),
        in_specs=[
            pl.BlockSpec((INPROJ_TM, D_MODEL), lambda i, j: (i, 0)),
            pl.BlockSpec((1, 1, COND_ROWS, D_MODEL), lambda i, j: (layer, 0, 0, 0)),
            pl.BlockSpec((1, 1, COND_ROWS, D_MODEL), lambda i, j: (layer, 1, 0, 0)),
            pl.BlockSpec((1, D_MODEL), lambda i, j: (0, 0)),
            pl.BlockSpec((D_MODEL, INPROJ_TN), lambda i, j: (0, j)),
            pl.BlockSpec((1, HEAD_DIM), lambda i, j: (0, 0)),
            pl.BlockSpec((1, HEAD_DIM), lambda i, j: (0, 0)),
            pl.BlockSpec((INPROJ_TM, HEAD_DIM), rope_map),
            pl.BlockSpec((INPROJ_TM, HEAD_DIM), rope_map),
        ],
        out_specs=(pl.BlockSpec((INPROJ_TM, INPROJ_TN), lambda i, j: (i, j)),
                   pl.BlockSpec((INPROJ_TM, INPROJ_TN), lambda i, j: (i, j))),
        scratch_shapes=[pltpu.VMEM((INPROJ_TM, D_MODEL), BF16)],
        compiler_params=_cparams("arbitrary", "arbitrary"),
        name="norm_inproj",
    )(x, mod, mod, g1, w_in_b, g_q, g_k, cos2, sin2)


POOL_HALO = 8


def _pool_kernel(up_ref, uc_ref, un_ref, wp_ref, ps_ref, o_ref, pad_scr, *, n_ctx):
    i = pl.program_id(0)
    row0 = i * ATT_TILE
    is_ctx = row0 < n_ctx
    t0 = jnp.where(is_ctx, 0, (row0 - n_ctx) % LAT_SEQ)
    seq = jnp.where(is_ctx, CTX_SEQ, LAT_SEQ)
    first = t0 == 0
    last = t0 + ATT_TILE == seq
    zeros = jnp.zeros((POOL_HALO, POOL_WIDTH), F32)
    pad_scr[0:POOL_HALO, :] = jnp.where(first, zeros, up_ref[ATT_TILE - POOL_HALO:ATT_TILE, :])
    pad_scr[POOL_HALO:POOL_HALO + ATT_TILE, :] = uc_ref[...]
    pad_scr[POOL_HALO + ATT_TILE:, :] = jnp.where(last, zeros, un_ref[0:POOL_HALO, :])
    t = t0 + lax.broadcasted_iota(I32, (ATT_TILE, 1), 0)
    for g, w in enumerate(POOL_WINDOWS):
        cols = slice(g * POOL_GROUP, (g + 1) * POOL_GROUP)
        acc = pad_scr[pl.ds(POOL_HALO - w // 2, ATT_TILE), cols]
        for d in range(-w // 2 + 1, w // 2):
            acc = acc + pad_scr[pl.ds(POOL_HALO + d, ATT_TILE), cols]
        cnt = (jnp.minimum(t + w // 2, seq) - jnp.maximum(t - w // 2, 0)).astype(F32)
        pooled = acc / cnt - uc_ref[:, cols]
        y = jnp.dot(pooled.astype(BF16), wp_ref[g], preferred_element_type=F32)
        o_ref[:, cols] = (y * ps_ref[:, cols]).astype(BF16)


def _pool(proj_f, w_pool_b, pool_scale, n_ctx):
    n_tok = proj_f.shape[0]
    nt = n_tok // ATT_TILE
    return pl.pallas_call(
        functools.partial(_pool_kernel, n_ctx=n_ctx),
        out_shape=jax.ShapeDtypeStruct((n_tok, POOL_WIDTH), BF16),
        grid=(nt,),
        in_specs=[
            pl.BlockSpec((ATT_TILE, POOL_WIDTH), lambda i: (jnp.maximum(i - 1, 0), 0)),
            pl.BlockSpec((ATT_TILE, POOL_WIDTH), lambda i: (i, 0)),
            pl.BlockSpec((ATT_TILE, POOL_WIDTH), lambda i: (jnp.minimum(i + 1, nt - 1), 0)),
            pl.BlockSpec((len(POOL_WINDOWS), POOL_GROUP, POOL_GROUP), lambda i: (0, 0, 0)),
            pl.BlockSpec((1, POOL_WIDTH), lambda i: (0, 0)),
        ],
        out_specs=pl.BlockSpec((ATT_TILE, POOL_WIDTH), lambda i: (i, 0)),
        scratch_shapes=[pltpu.VMEM((ATT_TILE + 2 * POOL_HALO, POOL_WIDTH), F32)],
        compiler_params=_cparams("arbitrary"),
        name="pool_mixer",
    )(proj_f, proj_f, proj_f, w_pool_b, pool_scale)


_NT_DIMS = (((1,), (1,)), ((), ()))
ATT_SCALE = HEAD_DIM ** -0.5


def _scores(q, k):
    return lax.dot_general(q, k, _NT_DIMS, preferred_element_type=F32) * ATT_SCALE


def _attend_one(q, k, v):
    s = _scores(q, k)
    p = jnp.exp(s - jnp.max(s, axis=-1, keepdims=True))
    o = jnp.dot(p.astype(BF16), v, preferred_element_type=F32)
    return o / jnp.sum(p, axis=-1, keepdims=True)


def _attend_two(s1, v1, s2, v2):
    m = jnp.maximum(jnp.max(s1, axis=-1, keepdims=True), jnp.max(s2, axis=-1, keepdims=True))
    p1 = jnp.exp(s1 - m)
    p2 = jnp.exp(s2 - m)
    o = (jnp.dot(p1.astype(BF16), v1, preferred_element_type=F32)
         + jnp.dot(p2.astype(BF16), v2, preferred_element_type=F32))
    return o / (jnp.sum(p1, axis=-1, keepdims=True) + jnp.sum(p2, axis=-1, keepdims=True))


def _head(h):
    return slice(h * HEAD_DIM, (h + 1) * HEAD_DIM)


def _na_kernel(q_ref, kc_ref, vc_ref, kl_ref, vl_ref, ck_ref, cv_ref, bias_ref, o_ref, *, n_ctx_steps):
    i = pl.program_id(0)

    @pl.when(i < n_ctx_steps)
    def _():
        for h in range(NA_HEADS):
            o = _attend_one(q_ref[:, _head(h)], kc_ref[:, _head(h)], vc_ref[:, _head(h)])
            o_ref[:, _head(h)] = o.astype(BF16)

    @pl.when(i >= n_ctx_steps)
    def _():
        tile = (i - n_ctx_steps) % (LAT_SEQ // ATT_TILE)
        win_row = jnp.clip(tile * NA_TILE_ROWS - NA_ROWS // 2, 0, GRID_ROWS - NA_WIN_ROWS)
        key0 = pl.multiple_of(win_row * GRID_W, GRID_W)
        for h in range(NA_HEADS):
            q = q_ref[:, _head(h)]
            s_lat = _scores(q, kl_ref[pl.ds(key0, NA_WIN), _head(h)]) + bias_ref[0, h]
            s_ctx = _scores(q, ck_ref[0, 0, :, _head(h)])
            o = _attend_two(s_lat, vl_ref[pl.ds(key0, NA_WIN), _head(h)],
                            s_ctx, cv_ref[0, 0, :, _head(h)])
            o_ref[:, _head(h)] = o.astype(BF16)


def _na_bias_tables(rel_bias):
    half = NA_ROWS // 2
    r0 = np.array([0, half, GRID_ROWS - NA_TILE_ROWS])[:, None, None]
    win = np.clip(r0 - half, 0, GRID_ROWS - NA_WIN_ROWS)
    qi = np.arange(ATT_TILE)[None, :, None]
    ki = np.arange(NA_WIN)[None, None, :]
    qr, qc = r0 + qi // GRID_W, qi % GRID_W
    kr, kc = win + ki // GRID_W, ki % GRID_W
    rs = np.clip(qr - half, 0, GRID_ROWS - NA_ROWS)
    cs = np.clip(qc - NA_COLS // 2, 0, GRID_W - NA_COLS)
    ok = (kr >= rs) & (kr < rs + NA_ROWS) & (kc >= cs) & (kc < cs + NA_COLS)
    dr = np.clip(kr - qr + NA_ROWS - 1, 0, 2 * NA_ROWS - 2)
    dc = np.clip(kc - qc + NA_COLS - 1, 0, 2 * NA_COLS - 2)
    vals = rel_bias[:, dr, dc]
    return jnp.where(ok[None], vals, NEG).transpose(1, 0, 2, 3).astype(F32)


def _na_attention(proj_b, cache_k, cache_v, bias, layer, n_ctx):
    n_tok = proj_b.shape[0]
    n_ctx_steps = n_ctx // ATT_TILE
    lat_tiles = LAT_SEQ // ATT_TILE
    lat_blk0 = n_ctx // LAT_SEQ
    wq = NA_WIDTH

    def ctx_blk(i):
        return jnp.minimum(i, n_ctx_steps - 1)

    def req(i):
        return jnp.maximum(i - n_ctx_steps, 0) // lat_tiles

    def kind(i):
        tile = jnp.maximum(i - n_ctx_steps, 0) % lat_tiles
        return jnp.where(tile == 0, 0, jnp.where(tile == lat_tiles - 1, 2, 1))

    return pl.pallas_call(
        functools.partial(_na_kernel, n_ctx_steps=n_ctx_steps),
        out_shape=jax.ShapeDtypeStruct((n_tok, NA_WIDTH), BF16),
        grid=(n_tok // ATT_TILE,),
        in_specs=[
            pl.BlockSpec((ATT_TILE, wq), lambda i: (i, COL_QN // wq)),
            pl.BlockSpec((ATT_TILE, wq), lambda i: (ctx_blk(i), COL_KN // wq)),
            pl.BlockSpec((ATT_TILE, wq), lambda i: (ctx_blk(i), COL_VN // wq)),
            pl.BlockSpec((LAT_SEQ, wq), lambda i: (lat_blk0 + req(i), COL_KN // wq)),
            pl.BlockSpec((LAT_SEQ, wq), lambda i: (lat_blk0 + req(i), COL_VN // wq)),
            pl.BlockSpec((1, 1, CTX_SEQ, wq), lambda i: (req(i), layer, 0, 0)),
            pl.BlockSpec((1, 1, CTX_SEQ, wq), lambda i: (req(i), layer, 0, 0)),
            pl.BlockSpec((1, NA_HEADS, ATT_TILE, NA_WIN), lambda i: (kind(i), 0, 0, 0)),
        ],
        out_specs=pl.BlockSpec((ATT_TILE, wq), lambda i: (i, 0)),
        compiler_params=_cparams("arbitrary"),
        name="na_attention",
    )(proj_b, proj_b, proj_b, proj_b, proj_b, cache_k, cache_v, bias)


def _gqa_kernel(q_ref, kc_ref, vc_ref, kl_ref, vl_ref, ck_ref, cv_ref, o_ref, *, n_ctx_steps):
    i = pl.program_id(0)

    @pl.when(i < n_ctx_steps)
    def _():
        for g in range(GQA_GROUP):
            o = _attend_one(q_ref[:, _head(g)], kc_ref[...], vc_ref[...])
            o_ref[:, _head(g)] = o.astype(BF16)

    @pl.when(i >= n_ctx_steps)
    def _():
        for g in range(GQA_GROUP):
            q = q_ref[:, _head(g)]
            o = _attend_two(_scores(q, kl_ref[...]), vl_ref[...],
                            _scores(q, ck_ref[0, 0]), cv_ref[0, 0])
            o_ref[:, _head(g)] = o.astype(BF16)


def _gqa_attention(proj_b, cache_k, cache_v, layer, n_ctx):
    n_tok = proj_b.shape[0]
    n_ctx_steps = n_ctx // ATT_TILE
    lat_tiles = LAT_SEQ // ATT_TILE
    lat_blk0 = n_ctx // LAT_SEQ
    wq = GQA_GROUP * HEAD_DIM
    kcol, vcol = COL_KG // HEAD_DIM, COL_VG // HEAD_DIM

    def ctx_blk(i):
        return jnp.minimum(i, n_ctx_steps - 1)

    def req(i):
        return jnp.maximum(i - n_ctx_steps, 0) // lat_tiles

    return pl.pallas_call(
        functools.partial(_gqa_kernel, n_ctx_steps=n_ctx_steps),
        out_shape=jax.ShapeDtypeStruct((n_tok, GQA_WIDTH), BF16),
        grid=(n_tok // ATT_TILE, GQA_KV_HEADS),
        in_specs=[
            pl.BlockSpec((ATT_TILE, wq), lambda i, kh: (i, COL_QG // wq + kh)),
            pl.BlockSpec((ATT_TILE, HEAD_DIM), lambda i, kh: (ctx_blk(i), kcol + kh)),
            pl.BlockSpec((ATT_TILE, HEAD_DIM), lambda i, kh: (ctx_blk(i), vcol + kh)),
            pl.BlockSpec((LAT_SEQ, HEAD_DIM), lambda i, kh: (lat_blk0 + req(i), kcol + kh)),
            pl.BlockSpec((LAT_SEQ, HEAD_DIM), lambda i, kh: (lat_blk0 + req(i), vcol + kh)),
            pl.BlockSpec((1, 1, CTX_SEQ, HEAD_DIM), lambda i, kh: (req(i), layer, 0, kh)),
            pl.BlockSpec((1, 1, CTX_SEQ, HEAD_DIM), lambda i, kh: (req(i), layer, 0, kh)),
        ],
        out_specs=pl.BlockSpec((ATT_TILE, wq), lambda i, kh: (i, kh)),
        compiler_params=_cparams("arbitrary", "arbitrary"),
        name="gqa_attention",
    )(proj_b, proj_b, proj_b, proj_b, proj_b, cache_k, cache_v)


def _outproj_kernel(x_ref, ya_ref, yb_ref, yc_ref, wo_ref, gate_ref, shift_ref, scale_ref, g2_ref,
                    wr_ref, br_ref, xo_ref, h_ref, lg_ref, *, n_ctx):
    crow = _cond_row(pl.program_id(0), OUTPROJ_TM, n_ctx)
    mix = (jnp.dot(ya_ref[...], wo_ref[0:POOL_WIDTH, :], preferred_element_type=F32)
           + jnp.dot(yb_ref[...], wo_ref[POOL_WIDTH:POOL_WIDTH + NA_WIDTH, :], preferred_element_type=F32)
           + jnp.dot(yc_ref[...], wo_ref[POOL_WIDTH + NA_WIDTH:, :], preferred_element_type=F32))
    x = x_ref[...] + gate_ref[0, 0, pl.ds(crow, 1), :] * mix
    xo_ref[...] = x
    y = x * lax.rsqrt(jnp.mean(x * x, axis=-1, keepdims=True) + EPS) * g2_ref[...]
    h = y * (1 + scale_ref[0, 0, pl.ds(crow, 1), :]) + shift_ref[0, 0, pl.ds(crow, 1), :]
    h_ref[...] = h
    lg_ref[...] = jnp.dot(h.astype(BF16), wr_ref[...], preferred_element_type=F32) + br_ref[...]


def _outproj(x, ya, yb, yc, w_out_b, mod, layer, g2, w_router_b, b_router_p, n_ctx):
    n_tok = x.shape[0]
    tm = OUTPROJ_TM

    def mod_spec(chunk):
        return pl.BlockSpec((1, 1, COND_ROWS, D_MODEL), lambda i: (layer, chunk, 0, 0))

    return pl.pallas_call(
        functools.partial(_outproj_kernel, n_ctx=n_ctx),
        out_shape=(jax.ShapeDtypeStruct((n_tok, D_MODEL), F32),
                   jax.ShapeDtypeStruct((n_tok, D_MODEL), F32),
                   jax.ShapeDtypeStruct((n_tok, LANES), F32)),
        grid=(n_tok // tm,),
        in_specs=[
            pl.BlockSpec((tm, D_MODEL), lambda i: (i, 0)),
            pl.BlockSpec((tm, POOL_WIDTH), lambda i: (i, 0)),
            pl.BlockSpec((tm, NA_WIDTH), lambda i: (i, 0)),
            pl.BlockSpec((tm, GQA_WIDTH), lambda i: (i, 0)),
            pl.BlockSpec((D_MODEL, D_MODEL), lambda i: (0, 0)),
            mod_spec(2), mod_spec(3), mod_spec(4),
            pl.BlockSpec((1, D_MODEL), lambda i: (0, 0)),
            pl.BlockSpec((D_MODEL, LANES), lambda i: (0, 0)),
            pl.BlockSpec((1, LANES), lambda i: (0, 0)),
        ],
        out_specs=(pl.BlockSpec((tm, D_MODEL), lambda i: (i, 0)),
                   pl.BlockSpec((tm, D_MODEL), lambda i: (i, 0)),
                   pl.BlockSpec((tm, LANES), lambda i: (i, 0))),
        compiler_params=_cparams("arbitrary"),
        name="outproj_norm_router",
    )(x, ya, yb, yc, w_out_b, mod, mod, mod, g2, w_router_b, b_router_p)


def _route_kernel(lg_ref, idx_ref, gate_ref, pos_ref, cnt_ref, carry_scr):
    i = pl.program_id(0)

    @pl.when(i == 0)
    def _():
        carry_scr[...] = jnp.zeros_like(carry_scr)

    tm = lg_ref.shape[0]
    l = lg_ref[...]
    lane = lax.broadcasted_iota(I32, (tm, LANES), 1)
    sel = jnp.zeros((tm, LANES), F32)
    vals, idxs = [], []
    for _ in range(TOP_K):
        m = jnp.max(l, axis=-1, keepdims=True)
        ik = jnp.min(jnp.where(l == m, lane, LANES), axis=-1, keepdims=True)
        hit = lane == ik
        sel = jnp.where(hit, 1.0, sel)
        l = jnp.where(hit, -jnp.inf, l)
        vals.append(m)
        idxs.append(ik)
    exps = [jnp.exp(v - vals[0]) for v in vals]
    tot = exps[0] + exps[1] + exps[2] + exps[3]
    r = lax.broadcasted_iota(I32, (tm, tm), 0)
    c = lax.broadcasted_iota(I32, (tm, tm), 1)
    tri = jnp.where(c < r, 1.0, 0.0).astype(BF16)
    before = jnp.dot(tri, sel.astype(BF16), preferred_element_type=F32) + carry_scr[...]
    idx_out = jnp.zeros((tm, LANES), I32)
    gate_out = jnp.zeros((tm, LANES), F32)
    pos_out = jnp.zeros((tm, LANES), I32)
    for k in range(TOP_K):
        pk = jnp.sum(jnp.where(lane == idxs[k], before, 0.0), axis=-1, keepdims=True)
        idx_out = jnp.where(lane == k, idxs[k], idx_out)
        gate_out = jnp.where(lane == k, exps[k] / tot, gate_out)
        pos_out = jnp.where(lane == k, pk.astype(I32), pos_out)
    idx_ref[...] = idx_out
    gate_ref[...] = gate_out
    pos_ref[...] = pos_out
    carry_scr[...] = carry_scr[...] + jnp.sum(sel, axis=0, keepdims=True)
    cnt_ref[...] = jnp.broadcast_to(carry_scr[...], cnt_ref.shape)


def _route(logits):
    n_tok = logits.shape[0]
    tm = ROUTE_TM
    tile = pl.BlockSpec((tm, LANES), lambda i: (i, 0))
    return pl.pallas_call(
        _route_kernel,
        out_shape=(jax.ShapeDtypeStruct((n_tok, LANES), I32),
                   jax.ShapeDtypeStruct((n_tok, LANES), F32),
                   jax.ShapeDtypeStruct((n_tok, LANES), I32),
                   jax.ShapeDtypeStruct((8, LANES), F32)),
        grid=(n_tok // tm,),
        in_specs=[tile],
        out_specs=(tile, tile, tile, pl.BlockSpec((8, LANES), lambda i: (0, 0))),
        scratch_shapes=[pltpu.VMEM((1, LANES), F32)],
        compiler_params=_cparams("arbitrary"),
        name="route_topk",
    )(logits)


ZERO_ROWS = 64


def _dispatch_kernel(cnt_ref, start_ref, nused_ref, dest_ref, h_hbm, xs_hbm, zero_scr, sem, *, n_blocks):
    i = pl.program_id(0)
    last = pl.num_programs(0) - 1

    def row_copy(src_row, dst_row):
        return pltpu.make_async_copy(h_hbm.at[pl.ds(src_row, 1)], xs_hbm.at[pl.ds(dst_row, 1)], sem)

    def zero_row_copy(dst_row):
        return pltpu.make_async_copy(zero_scr.at[pl.ds(0, 1)], xs_hbm.at[pl.ds(dst_row, 1)], sem)

    def zero_chunk_copy(dst_row):
        return pltpu.make_async_copy(zero_scr, xs_hbm.at[pl.ds(dst_row, ZERO_ROWS)], sem)

    def issue(t, carry):
        for k in range(TOP_K):
            row_copy(i * TOKEN_TILE + t, dest_ref[0, 0, t * TOP_K + k]).start()
        return carry

    lax.fori_loop(0, TOKEN_TILE, issue, 0)

    def drain(t, carry):
        for k in range(TOP_K):
            row_copy(0, 0).wait()
        return carry

    lax.fori_loop(0, TOKEN_TILE, drain, 0)

    @pl.when(i == last)
    def _():
        zero_scr[...] = jnp.zeros_like(zero_scr)

        def per_expert(e, carry):
            cnt = cnt_ref[e]
            padded = (cnt + MOE_BLOCK - 1) // MOE_BLOCK * MOE_BLOCK
            base = start_ref[e]

            def z_issue(p, c):
                zero_row_copy(base + p).start()
                return c

            def z_drain(p, c):
                zero_row_copy(0).wait()
                return c

            lax.fori_loop(cnt, padded, z_issue, 0)
            lax.fori_loop(cnt, padded, z_drain, 0)
            return carry

        lax.fori_loop(0, N_EXPERTS, per_expert, 0)

        chunks = MOE_BLOCK // ZERO_ROWS

        def t_issue(c, carry):
            zero_chunk_copy(c * ZERO_ROWS).start()
            return carry

        def t_drain(c, carry):
            zero_chunk_copy(0).wait()
            return carry

        lax.fori_loop(nused_ref[0] * chunks, n_blocks * chunks, t_issue, 0)
        lax.fori_loop(nused_ref[0] * chunks, n_blocks * chunks, t_drain, 0)


def _dispatch(counts, starts, nused, dest, h, n_blocks):
    n_tok = h.shape[0]
    nt = n_tok // TOKEN_TILE
    grid_spec = pltpu.PrefetchScalarGridSpec(
        num_scalar_prefetch=3,
        grid=(nt,),
        in_specs=[
            pl.BlockSpec((1, 1, TOKEN_TILE * TOP_K), lambda i, *_: (i, 0, 0), memory_space=pltpu.SMEM),
            pl.BlockSpec(memory_space=pl.ANY),
        ],
        out_specs=pl.BlockSpec(memory_space=pl.ANY),
        scratch_shapes=[pltpu.VMEM((ZERO_ROWS, D_MODEL), F32), pltpu.SemaphoreType.DMA],
    )
    return pl.pallas_call(
        functools.partial(_dispatch_kernel, n_blocks=n_blocks),
        out_shape=jax.ShapeDtypeStruct((n_blocks * MOE_BLOCK, D_MODEL), F32),
        grid_spec=grid_spec,
        compiler_params=_cparams("arbitrary"),
        name="moe_dispatch",
    )(counts, starts, nused, dest.reshape(nt, 1, TOKEN_TILE * TOP_K), h)


def _expert_kernel(blk_e_ref, nused_ref, xs_ref, wgu_ref, bgu_ref, wd_ref, bd_ref, ys_ref, wgu_scr, wd_scr):
    b = pl.program_id(0)
    e = blk_e_ref[b]
    prev = blk_e_ref[jnp.maximum(b - 1, 0)]

    @pl.when((b == 0) | (e != prev))
    def _():
        wgu_scr[...] = wgu_ref[0].astype(BF16)
        wd_scr[...] = wd_ref[0].astype(BF16)

    @pl.when(b < nused_ref[0])
    def _():
        gu = jnp.dot(xs_ref[...].astype(BF16), wgu_scr[...], preferred_element_type=F32) + bgu_ref[0]
        gate = jnp.minimum(gu[:, :D_FF], SWIGLU_LIMIT)
        up = jnp.clip(gu[:, D_FF:], -SWIGLU_LIMIT, SWIGLU_LIMIT)
        act = (up + 1) * (gate * jax.nn.sigmoid(SWIGLU_ALPHA * gate))
        ys_ref[...] = jnp.dot(act.astype(BF16), wd_scr[...], preferred_element_type=F32) + bd_ref[0]

    @pl.when(b >= nused_ref[0])
    def _():
        ys_ref[...] = jnp.zeros_like(ys_ref)


def _experts(blk_e, nused, xs, w_gate_up, b_gate_up, w_down, b_down):
    n_blocks = blk_e.shape[0]
    grid_spec = pltpu.PrefetchScalarGridSpec(
        num_scalar_prefetch=2,
        grid=(n_blocks,),
        in_specs=[
            pl.BlockSpec((MOE_BLOCK, D_MODEL), lambda b, be, nu: (jnp.minimum(b, nu[0] - 1), 0)),
            pl.BlockSpec((1, D_MODEL, 2 * D_FF), lambda b, be, nu: (be[b], 0, 0)),
            pl.BlockSpec((1, 1, 2 * D_FF), lambda b, be, nu: (be[b], 0, 0)),
            pl.BlockSpec((1, D_FF, D_MODEL), lambda b, be, nu: (be[b], 0, 0)),
            pl.BlockSpec((1, 1, D_MODEL), lambda b, be, nu: (be[b], 0, 0)),
        ],
        out_specs=pl.BlockSpec((MOE_BLOCK, D_MODEL), lambda b, be, nu: (b, 0)),
        scratch_shapes=[pltpu.VMEM((D_MODEL, 2 * D_FF), BF16), pltpu.VMEM((D_FF, D_MODEL), BF16)],
    )
    return pl.pallas_call(
        _expert_kernel,
        out_shape=jax.ShapeDtypeStruct(xs.shape, F32),
        grid_spec=grid_spec,
        compiler_params=_cparams("arbitrary"),
        name="moe_experts",
    )(blk_e, nused, xs, w_gate_up, b_gate_up.reshape(N_EXPERTS, 1, 2 * D_FF),
      w_down, b_down.reshape(N_EXPERTS, 1, D_MODEL))


def _combine_kernel(dest_ref, gate_ref, x_ref, mgate_ref, ys_hbm, o_ref, rows_scr, sem, *, n_ctx):
    crow = _cond_row(pl.program_id(0), TOKEN_TILE, n_ctx)

    def row_copy(src_row, k, t):
        return pltpu.make_async_copy(ys_hbm.at[pl.ds(src_row, 1)], rows_scr.at[k, pl.ds(t, 1)], sem)

    def issue(t, carry):
        for k in range(TOP_K):
            row_copy(dest_ref[0, 0, t * TOP_K + k], k, t).start()
        return carry

    lax.fori_loop(0, TOKEN_TILE, issue, 0)

    def drain(t, carry):
        for k in range(TOP_K):
            row_copy(0, 0, 0).wait()
        return carry

    lax.fori_loop(0, TOKEN_TILE, drain, 0)

    g = gate_ref[...]
    y = g[:, 0:1] * rows_scr[0]
    for k in range(1, TOP_K):
        y = y + g[:, k:k + 1] * rows_scr[k]
    o_ref[...] = x_ref[...] + mgate_ref[0, 0, pl.ds(crow, 1), :] * y


def _combine(dest, gates, x, mod, layer, ys, n_ctx):
    n_tok = x.shape[0]
    nt = n_tok // TOKEN_TILE
    return pl.pallas_call(
        functools.partial(_combine_kernel, n_ctx=n_ctx),
        out_shape=jax.ShapeDtypeStruct((n_tok, D_MODEL), F32),
        grid=(nt,),
        in_specs=[
            pl.BlockSpec((1, 1, TOKEN_TILE * TOP_K), lambda i: (i, 0, 0), memory_space=pltpu.SMEM),
            pl.BlockSpec((TOKEN_TILE, LANES), lambda i: (i, 0)),
            pl.BlockSpec((TOKEN_TILE, D_MODEL), lambda i: (i, 0)),
            pl.BlockSpec((1, 1, COND_ROWS, D_MODEL), lambda i: (layer, 5, 0, 0)),
            pl.BlockSpec(memory_space=pl.ANY),
        ],
        out_specs=pl.BlockSpec((TOKEN_TILE, D_MODEL), lambda i: (i, 0)),
        scratch_shapes=[pltpu.VMEM((TOP_K, TOKEN_TILE, D_MODEL), F32), pltpu.SemaphoreType.DMA],
        compiler_params=_cparams("arbitrary"),
        name="moe_combine",
    )(dest.reshape(nt, 1, TOKEN_TILE * TOP_K), gates, x, mod, ys)


def _final_norm_kernel(x_ref, g_ref, o_ref):
    x = x_ref[...]
    o_ref[...] = x * lax.rsqrt(jnp.mean(x * x, axis=-1, keepdims=True) + EPS) * g_ref[...]


def _final_norm(x, g):
    n_tok = x.shape[0]
    tm = OUTPROJ_TM
    return pl.pallas_call(
        _final_norm_kernel,
        out_shape=jax.ShapeDtypeStruct((n_tok, D_MODEL), F32),
        grid=(n_tok // tm,),
        in_specs=[pl.BlockSpec((tm, D_MODEL), lambda i: (i, 0)), pl.BlockSpec((1, D_MODEL), lambda i: (0, 0))],
        out_specs=pl.BlockSpec((tm, D_MODEL), lambda i: (i, 0)),
        compiler_params=_cparams("arbitrary"),
        name="final_norm",
    )(x, g)


def _rope_tables():
    t = np.arange(LAT_SEQ)
    row = (t // GRID_W).astype(np.float32)
    col = (t % GRID_W).astype(np.float32)
    half = HEAD_DIM // 2
    inv = jnp.asarray(ROPE_THETA, F32) ** (-jnp.arange(0, half, 2, dtype=F32) / half)
    ang = jnp.concatenate([row[:, None] * inv, col[:, None] * inv], axis=-1)
    cos, sin = jnp.cos(ang), jnp.sin(ang)
    return jnp.concatenate([cos, cos], axis=-1), jnp.concatenate([-sin, sin], axis=-1)


def _routing_plan(idx, pos, counts_f, n_blocks):
    counts = counts_f[0, :N_EXPERTS].astype(I32)
    padded = (counts + MOE_BLOCK - 1) // MOE_BLOCK * MOE_BLOCK
    pend = jnp.cumsum(padded)
    starts = pend - padded
    dest = starts[idx] + pos
    nused = pend[-1] // MOE_BLOCK
    blk = jnp.arange(n_blocks, dtype=I32)
    blk_e = jnp.clip(jnp.searchsorted(pend, blk * MOE_BLOCK, side="right"), 0, N_EXPERTS - 1).astype(I32)
    blk_e = jnp.where(blk < nused, blk_e, blk_e[nused - 1])
    return counts, starts.astype(I32), nused.reshape(1).astype(I32), dest.astype(I32), blk_e


def _forward(x_prompt, x_sample, cache_na_k, cache_na_v, cache_gqa_k, cache_gqa_v, c, c_ctx,
             w_ada, b_ada, g_norm1, g_norm2, w_in, w_pool, pool_scale, na_rel_bias, g_q, g_k,
             w_out, w_router, b_router, w_gate_up, b_gate_up, w_down, b_down, g_final):
    depth = w_in.shape[0]
    nb_ctx, nb_lat = x_prompt.shape[0], x_sample.shape[0]
    n_ctx, n_lat = nb_ctx * CTX_SEQ, nb_lat * LAT_SEQ
    n_tok = n_ctx + n_lat
    assert x_prompt.shape[1:] == (CTX_SEQ, D_MODEL) and x_sample.shape[1:] == (LAT_SEQ, D_MODEL)
    assert n_ctx % LAT_SEQ == 0 and 1 + nb_lat <= COND_ROWS
    n_blocks = n_tok * TOP_K // MOE_BLOCK + N_EXPERTS

    x = jnp.concatenate([x_prompt.reshape(n_ctx, D_MODEL), x_sample.reshape(n_lat, D_MODEL)], axis=0)
    cond = jnp.zeros((COND_ROWS, D_MODEL), F32).at[0].set(c_ctx).at[1:1 + nb_lat].set(c)
    mod = _ada_mod(cond, w_ada, b_ada)

    cos2, sin2 = _rope_tables()
    w_in_b = w_in.astype(BF16)
    w_out_b = w_out.astype(BF16)
    w_pool_b = w_pool.astype(BF16)
    w_router_b = jnp.pad(w_router, ((0, 0), (0, 0), (0, LANES - N_EXPERTS))).astype(BF16)
    b_router_p = jnp.pad(b_router, ((0, 0), (0, LANES - N_EXPERTS)), constant_values=NEG)
    cna_k = cache_na_k.reshape(nb_lat, depth, CTX_SEQ, NA_WIDTH).astype(BF16)
    cna_v = cache_na_v.reshape(nb_lat, depth, CTX_SEQ, NA_WIDTH).astype(BF16)
    cgq_k = cache_gqa_k.reshape(nb_lat, depth, CTX_SEQ, GQA_KV_WIDTH).astype(BF16)
    cgq_v = cache_gqa_v.reshape(nb_lat, depth, CTX_SEQ, GQA_KV_WIDTH).astype(BF16)

    new_kv = []
    for l in range(depth):
        proj_f, proj_b = _inproj(x, mod, l, g_norm1[l][None], w_in_b[l], g_q[l][None], g_k[l][None],
                                 cos2, sin2, n_ctx)
        new_kv.append(proj_f[:n_ctx])
        ya = _pool(proj_f, w_pool_b[l], pool_scale[l][None], n_ctx)
        yb = _na_attention(proj_b, cna_k, cna_v, _na_bias_tables(na_rel_bias[l]), l, n_ctx)
        yc = _gqa_attention(proj_b, cgq_k, cgq_v, l, n_ctx)
        x, h, logits = _outproj(x, ya, yb, yc, w_out_b[l], mod, l, g_norm2[l][None],
                                w_router_b[l], b_router_p[l][None], n_ctx)
        idx, gates, pos, counts_f = _route(logits)
        counts, starts, nused, dest, blk_e = _routing_plan(idx[:, :TOP_K], pos[:, :TOP_K], counts_f, n_blocks)
        xs = _dispatch(counts, starts, nused, dest, h, n_blocks)
        ys = _experts(blk_e, nused, xs, w_gate_up[l], b_gate_up[l], w_down[l], b_down[l])
        x = _combine(dest, gates, x, mod, l, ys, n_ctx)

    y = _final_norm(x, g_final[None])
    y_prompt = y[:n_ctx].reshape(nb_ctx, CTX_SEQ, D_MODEL)
    y_sample = y[n_ctx:].reshape(nb_lat, LAT_SEQ, D_MODEL)

    def stack(col, width):
        per_layer = [p[:, col:col + width].reshape(nb_ctx, CTX_SEQ, width // HEAD_DIM, HEAD_DIM) for p in new_kv]
        return jnp.stack(per_layer, axis=1)

    return (y_prompt, y_sample, stack(COL_KN, NA_WIDTH), stack(COL_VN, NA_WIDTH),
            stack(COL_KG, GQA_KV_WIDTH), stack(COL_VG, GQA_KV_WIDTH))


def kernel(x_prompt, x_sample, cache_na_k, cache_na_v, cache_gqa_k, cache_gqa_v, c, c_ctx, w_ada, b_ada,
           g_norm1, g_norm2, w_in, w_pool, pool_scale, na_rel_bias, g_q, g_k, w_out, w_router, b_router,
           w_gate_up, b_gate_up, w_down, b_down, g_final):
    return _forward(x_prompt, x_sample, cache_na_k, cache_na_v, cache_gqa_k, cache_gqa_v, c, c_ctx,
                    w_ada, b_ada, g_norm1, g_norm2, w_in, w_pool, pool_scale, na_rel_bias, g_q, g_k,
                    w_out, w_router, b_router, w_gate_up, b_gate_up, w_down, b_down, g_final)
```

```python
import functools
import math

import numpy as np
import jax
import jax.numpy as jnp
from jax import lax
from jax.experimental import pallas as pl
from jax.experimental.pallas import tpu as pltpu

F32 = jnp.float32
BF16 = jnp.bfloat16
I32 = jnp.int32

D_MODEL = 2048
HEAD_DIM = 128
CTX_SEQ = 256
LAT_SEQ = 2048
GRID_W = 64
GRID_ROWS = LAT_SEQ // GRID_W
POOL_WINDOWS = (2, 4, 8, 16)
POOL_GROUP = 128
POOL_WIDTH = 512
NA_HEADS = 4
NA_WIDTH = 512
NA_ROWS = 8
NA_COLS = 16
GQA_HEADS = 8
GQA_KV_HEADS = 2
GQA_GROUP = GQA_HEADS // GQA_KV_HEADS
GQA_WIDTH = 1024
GQA_KV_WIDTH = 256
IN_WIDTH = 3584
N_EXPERTS = 32
TOP_K = 4
D_FF = 512
SWIGLU_LIMIT = 7.0
SWIGLU_ALPHA = 1.702
ROPE_THETA = 10000.0
EPS = 1e-6
NEG = -1e30

COL_U, COL_QN, COL_KN, COL_VN, COL_QG, COL_KG, COL_VG = 0, 512, 1024, 1536, 2048, 3072, 3328

LANES = 128
COND_ROWS = 16
ATT_TILE = 256
NA_TILE_ROWS = ATT_TILE // GRID_W
NA_WIN_ROWS = NA_ROWS + NA_TILE_ROWS - 1
NA_WIN = NA_WIN_ROWS * GRID_W
INPROJ_TM = 1024
INPROJ_TN = 512
OUTPROJ_TM = 512
ROUTE_TM = 512
MOE_BLOCK = 512
TOKEN_TILE = 256
SLOT_SHIFT = 16
SLOT_BASE = 1 << SLOT_SHIFT
DMA_UNROLL = 8
VMEM_LIMIT = 56 * 1024 * 1024


def _cparams(*sem):
    return pltpu.CompilerParams(dimension_semantics=sem, vmem_limit_bytes=VMEM_LIMIT)


def _cond_row(tile, tile_rows, n_ctx):
    row0 = tile * tile_rows
    return jnp.where(row0 < n_ctx, 0, 1 + (row0 - n_ctx) // LAT_SEQ)


def _ada_kernel(c_ref, w_ref, b_ref, o_ref):
    c = c_ref[...]
    s = (c * jax.nn.sigmoid(c)).astype(BF16)
    o_ref[0, 0] = jnp.dot(s, w_ref[0].astype(BF16), preferred_element_type=F32) + b_ref[0]


def _ada_mod(cond, w_ada, b_ada):
    depth = w_ada.shape[0]
    tn = 1024
    per = D_MODEL // tn
    return pl.pallas_call(
        _ada_kernel,
        out_shape=jax.ShapeDtypeStruct((depth, 6, COND_ROWS, D_MODEL), F32),
        grid=(depth, 6 * per),
        in_specs=[
            pl.BlockSpec((COND_ROWS, D_MODEL), lambda l, j: (0, 0)),
            pl.BlockSpec((1, D_MODEL, tn), lambda l, j: (l, 0, j)),
            pl.BlockSpec((1, 1, tn), lambda l, j: (l, 0, j)),
        ],
        out_specs=pl.BlockSpec((1, 1, COND_ROWS, tn), lambda l, j: (l, j // per, 0, j % per)),
        compiler_params=_cparams("arbitrary", "arbitrary"),
        name="ada_mod",
    )(cond, w_ada, b_ada.reshape(depth, 1, 6 * D_MODEL))


def _head_rms(x, g):
    return x * lax.rsqrt(jnp.mean(x * x, axis=-1, keepdims=True) + EPS) * g


def _inproj_kernel(x_ref, shift_ref, scale_ref, g1_ref, w_ref, gq_ref, gk_ref, cos_ref, sin_ref,
                   of_ref, ob_ref, h_scr, *, n_ctx):
    i = pl.program_id(0)
    j = pl.program_id(1)
    crow = _cond_row(i, INPROJ_TM, n_ctx)
    is_lat = i * INPROJ_TM >= n_ctx

    @pl.when(j == 0)
    def _():
        x = x_ref[...]
        y = x * lax.rsqrt(jnp.mean(x * x, axis=-1, keepdims=True) + EPS) * g1_ref[...]
        shift = shift_ref[0, 0, pl.ds(crow, 1), :]
        scale = scale_ref[0, 0, pl.ds(crow, 1), :]
        h_scr[...] = (y * (1 + scale) + shift).astype(BF16)

    acc = jnp.dot(h_scr[...], w_ref[0], preferred_element_type=F32)

    def rope(xh):
        cos = jnp.where(is_lat, cos_ref[...], 1.0)
        sin = jnp.where(is_lat, sin_ref[...], 0.0)
        return xh * cos + pltpu.roll(xh, HEAD_DIM // 2, 1) * sin

    def store(vals):
        for h, v in enumerate(vals):
            of_ref[:, h * HEAD_DIM:(h + 1) * HEAD_DIM] = v
            ob_ref[:, h * HEAD_DIM:(h + 1) * HEAD_DIM] = v.astype(BF16)

    heads = [acc[:, h * HEAD_DIM:(h + 1) * HEAD_DIM] for h in range(INPROJ_TN // HEAD_DIM)]
    jq0 = COL_QG // INPROJ_TN
    jk = COL_KG // INPROJ_TN

    @pl.when(j < jq0)
    def _():
        store(heads)

    @pl.when((j >= jq0) & (j < jk))
    def _():
        store([rope(_head_rms(xh, gq_ref[...])) for xh in heads])

    @pl.when(j == jk)
    def _():
        nk = GQA_KV_WIDTH // HEAD_DIM
        store([rope(_head_rms(xh, gk_ref[...])) for xh in heads[:nk]] + heads[nk:])


def _inproj(x, mod, layer, g1, w_in_b, g_q, g_k, cos2, sin2, n_ctx):
    n_tok = x.shape[0]
    lat_tiles = LAT_SEQ // INPROJ_TM
    ctx_tiles = n_ctx // INPROJ_TM

    def rope_map(i, j):
        return (jnp.maximum(i - ctx_tiles, 0) % lat_tiles, 0)

    return pl.pallas_call(
        functools.partial(_inproj_kernel, n_ctx=n_ctx),
        out_shape=(jax.ShapeDtypeStruct((n_tok, IN_WIDTH), F32),
                   jax.ShapeDtypeStruct((n_tok, IN_WIDTH), BF16)),
        grid=(n_tok // INPROJ_TM, IN_WIDTH // INPROJ_TN),
        in_specs=[
            pl.BlockSpec((INPROJ_TM, D_MODEL), lambda i, j: (i, 0)),
            pl.BlockSpec((1, 1, COND_ROWS, D_MODEL), lambda i, j: (layer, 0, 0, 0)),
            pl.BlockSpec((1, 1, COND_ROWS, D_MODEL), lambda i, j: (layer, 1, 0, 0)),
            pl.BlockSpec((1, D_MODEL), lambda i, j: (0, 0)),
            pl.BlockSpec((1, D_MODEL, INPROJ_TN), lambda i, j: (layer, 0, j)),
            pl.BlockSpec((1, HEAD_DIM), lambda i, j: (0, 0)),
            pl.BlockSpec((1, HEAD_DIM), lambda i, j: (0, 0)),
            pl.BlockSpec((INPROJ_TM, HEAD_DIM), rope_map),
            pl.BlockSpec((INPROJ_TM, HEAD_DIM), rope_map),
        ],
        out_specs=(pl.BlockSpec((INPROJ_TM, INPROJ_TN), lambda i, j: (i, j)),
                   pl.BlockSpec((INPROJ_TM, INPROJ_TN), lambda i, j: (i, j))),
        scratch_shapes=[pltpu.VMEM((INPROJ_TM, D_MODEL), BF16)],
        compiler_params=_cparams("arbitrary", "arbitrary"),
        name="norm_inproj",
    )(x, mod, mod, g1, w_in_b, g_q, g_k, cos2, sin2)


POOL_HALO = 8


def _pool_kernel(up_ref, uc_ref, un_ref, wp_ref, ps_ref, o_ref, pad_scr, *, n_ctx):
    i = pl.program_id(0)
    row0 = i * ATT_TILE
    is_ctx = row0 < n_ctx
    t0 = jnp.where(is_ctx, 0, (row0 - n_ctx) % LAT_SEQ)
    seq = jnp.where(is_ctx, CTX_SEQ, LAT_SEQ)
    first = t0 == 0
    last = t0 + ATT_TILE == seq
    zeros = jnp.zeros((POOL_HALO, POOL_WIDTH), F32)
    pad_scr[0:POOL_HALO, :] = jnp.where(first, zeros, up_ref[ATT_TILE - POOL_HALO:ATT_TILE, :])
    pad_scr[POOL_HALO:POOL_HALO + ATT_TILE, :] = uc_ref[...]
    pad_scr[POOL_HALO + ATT_TILE:, :] = jnp.where(last, zeros, un_ref[0:POOL_HALO, :])
    t = t0 + lax.broadcasted_iota(I32, (ATT_TILE, 1), 0)
    for g, w in enumerate(POOL_WINDOWS):
        cols = slice(g * POOL_GROUP, (g + 1) * POOL_GROUP)
        acc = pad_scr[pl.ds(POOL_HALO - w // 2, ATT_TILE), cols]
        for d in range(-w // 2 + 1, w // 2):
            acc = acc + pad_scr[pl.ds(POOL_HALO + d, ATT_TILE), cols]
        cnt = (jnp.minimum(t + w // 2, seq) - jnp.maximum(t - w // 2, 0)).astype(F32)
        pooled = acc / cnt - uc_ref[:, cols]
        y = jnp.dot(pooled.astype(BF16), wp_ref[g], preferred_element_type=F32)
        o_ref[:, cols] = (y * ps_ref[:, cols]).astype(BF16)


def _pool(proj_f, w_pool_b, pool_scale, n_ctx):
    n_tok = proj_f.shape[0]
    nt = n_tok // ATT_TILE
    return pl.pallas_call(
        functools.partial(_pool_kernel, n_ctx=n_ctx),
        out_shape=jax.ShapeDtypeStruct((n_tok, POOL_WIDTH), BF16),
        grid=(nt,),
        in_specs=[
            pl.BlockSpec((ATT_TILE, POOL_WIDTH), lambda i: (jnp.maximum(i - 1, 0), 0)),
            pl.BlockSpec((ATT_TILE, POOL_WIDTH), lambda i: (i, 0)),
            pl.BlockSpec((ATT_TILE, POOL_WIDTH), lambda i: (jnp.minimum(i + 1, nt - 1), 0)),
            pl.BlockSpec((len(POOL_WINDOWS), POOL_GROUP, POOL_GROUP), lambda i: (0, 0, 0)),
            pl.BlockSpec((1, POOL_WIDTH), lambda i: (0, 0)),
        ],
        out_specs=pl.BlockSpec((ATT_TILE, POOL_WIDTH), lambda i: (i, 0)),
        scratch_shapes=[pltpu.VMEM((ATT_TILE + 2 * POOL_HALO, POOL_WIDTH), F32)],
        compiler_params=_cparams("arbitrary"),
        name="pool_mixer",
    )(proj_f, proj_f, proj_f, w_pool_b, pool_scale)


_NT_DIMS = (((1,), (1,)), ((), ()))
ATT_SCALE = HEAD_DIM ** -0.5
EXP2_SCALE = ATT_SCALE * math.log2(math.e)


def _scores(q, k):
    return lax.dot_general(q, k, _NT_DIMS, preferred_element_type=F32)


def _attend_one(q, k, v):
    s = _scores(q, k)
    p = jnp.exp2((s - jnp.max(s, axis=-1, keepdims=True)) * EXP2_SCALE)
    o = jnp.dot(p.astype(BF16), v, preferred_element_type=F32)
    return o / jnp.sum(p, axis=-1, keepdims=True)


def _attend_two(s1, v1, s2, v2):
    m = jnp.maximum(jnp.max(s1, axis=-1, keepdims=True), jnp.max(s2, axis=-1, keepdims=True))
    p1 = jnp.exp2((s1 - m) * EXP2_SCALE)
    p2 = jnp.exp2((s2 - m) * EXP2_SCALE)
    o = (jnp.dot(p1.astype(BF16), v1, preferred_element_type=F32)
         + jnp.dot(p2.astype(BF16), v2, preferred_element_type=F32))
    return o / (jnp.sum(p1, axis=-1, keepdims=True) + jnp.sum(p2, axis=-1, keepdims=True))


def _head(h):
    return slice(h * HEAD_DIM, (h + 1) * HEAD_DIM)


def _na_kernel(q_ref, kc_ref, vc_ref, kl_ref, vl_ref, ck_ref, cv_ref, bias_ref, o_ref, *, n_ctx_steps):
    i = pl.program_id(0)

    @pl.when(i < n_ctx_steps)
    def _():
        for h in range(NA_HEADS):
            o = _attend_one(q_ref[:, _head(h)], kc_ref[:, _head(h)], vc_ref[:, _head(h)])
            o_ref[:, _head(h)] = o.astype(BF16)

    @pl.when(i >= n_ctx_steps)
    def _():
        tile = (i - n_ctx_steps) % (LAT_SEQ // ATT_TILE)
        win_row = jnp.clip(tile * NA_TILE_ROWS - NA_ROWS // 2, 0, GRID_ROWS - NA_WIN_ROWS)
        key0 = pl.multiple_of(win_row * GRID_W, GRID_W)
        for h in range(NA_HEADS):
            q = q_ref[:, _head(h)]
            s_lat = _scores(q, kl_ref[pl.ds(key0, NA_WIN), _head(h)]) + bias_ref[0, h]
            s_ctx = _scores(q, ck_ref[0, 0, :, _head(h)])
            o = _attend_two(s_lat, vl_ref[pl.ds(key0, NA_WIN), _head(h)],
                            s_ctx, cv_ref[0, 0, :, _head(h)])
            o_ref[:, _head(h)] = o.astype(BF16)


def _na_bias_tables(rel_bias):
    half = NA_ROWS // 2
    r0 = np.array([0, half, GRID_ROWS - NA_TILE_ROWS])[:, None, None]
    win = np.clip(r0 - half, 0, GRID_ROWS - NA_WIN_ROWS)
    qi = np.arange(ATT_TILE)[None, :, None]
    ki = np.arange(NA_WIN)[None, None, :]
    qr, qc = r0 + qi // GRID_W, qi % GRID_W
    kr, kc = win + ki // GRID_W, ki % GRID_W
    rs = np.clip(qr - half, 0, GRID_ROWS - NA_ROWS)
    cs = np.clip(qc - NA_COLS // 2, 0, GRID_W - NA_COLS)
    ok = (kr >= rs) & (kr < rs + NA_ROWS) & (kc >= cs) & (kc < cs + NA_COLS)
    dr = np.clip(kr - qr + NA_ROWS - 1, 0, 2 * NA_ROWS - 2)
    dc = np.clip(kc - qc + NA_COLS - 1, 0, 2 * NA_COLS - 2)
    vals = rel_bias[:, dr, dc]
    return jnp.where(ok[None], vals / ATT_SCALE, NEG).transpose(1, 0, 2, 3).astype(F32)


def _na_attention(proj_b, cache_k, cache_v, bias, layer, n_ctx):
    n_tok = proj_b.shape[0]
    n_ctx_steps = n_ctx // ATT_TILE
    lat_tiles = LAT_SEQ // ATT_TILE
    lat_blk0 = n_ctx // LAT_SEQ
    wq = NA_WIDTH

    def ctx_blk(i):
        return jnp.minimum(i, n_ctx_steps - 1)

    def req(i):
        return jnp.maximum(i - n_ctx_steps, 0) // lat_tiles

    def kind(i):
        tile = jnp.maximum(i - n_ctx_steps, 0) % lat_tiles
        return jnp.where(tile == 0, 0, jnp.where(tile == lat_tiles - 1, 2, 1))

    return pl.pallas_call(
        functools.partial(_na_kernel, n_ctx_steps=n_ctx_steps),
        out_shape=jax.ShapeDtypeStruct((n_tok, NA_WIDTH), BF16),
        grid=(n_tok // ATT_TILE,),
        in_specs=[
            pl.BlockSpec((ATT_TILE, wq), lambda i: (i, COL_QN // wq)),
            pl.BlockSpec((ATT_TILE, wq), lambda i: (ctx_blk(i), COL_KN // wq)),
            pl.BlockSpec((ATT_TILE, wq), lambda i: (ctx_blk(i), COL_VN // wq)),
            pl.BlockSpec((LAT_SEQ, wq), lambda i: (lat_blk0 + req(i), COL_KN // wq)),
            pl.BlockSpec((LAT_SEQ, wq), lambda i: (lat_blk0 + req(i), COL_VN // wq)),
            pl.BlockSpec((1, 1, CTX_SEQ, wq), lambda i: (req(i), layer, 0, 0)),
            pl.BlockSpec((1, 1, CTX_SEQ, wq), lambda i: (req(i), layer, 0, 0)),
            pl.BlockSpec((1, NA_HEADS, ATT_TILE, NA_WIN), lambda i: (kind(i), 0, 0, 0)),
        ],
        out_specs=pl.BlockSpec((ATT_TILE, wq), lambda i: (i, 0)),
        compiler_params=_cparams("arbitrary"),
        name="na_attention",
    )(proj_b, proj_b, proj_b, proj_b, proj_b, cache_k, cache_v, bias)


def _gqa_kernel(q_ref, kc_ref, vc_ref, kl_ref, vl_ref, ck_ref, cv_ref, o_ref, *, n_ctx_steps):
    i = pl.program_id(0)

    @pl.when(i < n_ctx_steps)
    def _():
        for g in range(GQA_GROUP):
            o = _attend_one(q_ref[:, _head(g)], kc_ref[...], vc_ref[...])
            o_ref[:, _head(g)] = o.astype(BF16)

    @pl.when(i >= n_ctx_steps)
    def _():
        for g in range(GQA_GROUP):
            q = q_ref[:, _head(g)]
            o = _attend_two(_scores(q, kl_ref[...]), vl_ref[...],
                            _scores(q, ck_ref[0, 0]), cv_ref[0, 0])
            o_ref[:, _head(g)] = o.astype(BF16)


def _gqa_attention(proj_b, cache_k, cache_v, layer, n_ctx):
    n_tok = proj_b.shape[0]
    n_ctx_steps = n_ctx // ATT_TILE
    lat_tiles = LAT_SEQ // ATT_TILE
    lat_blk0 = n_ctx // LAT_SEQ
    wq = GQA_GROUP * HEAD_DIM
    kcol, vcol = COL_KG // HEAD_DIM, COL_VG // HEAD_DIM

    def ctx_blk(i):
        return jnp.minimum(i, n_ctx_steps - 1)

    def req(i):
        return jnp.maximum(i - n_ctx_steps, 0) // lat_tiles

    return pl.pallas_call(
        functools.partial(_gqa_kernel, n_ctx_steps=n_ctx_steps),
        out_shape=jax.ShapeDtypeStruct((n_tok, GQA_WIDTH), BF16),
        grid=(n_tok // ATT_TILE, GQA_KV_HEADS),
        in_specs=[
            pl.BlockSpec((ATT_TILE, wq), lambda i, kh: (i, COL_QG // wq + kh)),
            pl.BlockSpec((ATT_TILE, HEAD_DIM), lambda i, kh: (ctx_blk(i), kcol + kh)),
            pl.BlockSpec((ATT_TILE, HEAD_DIM), lambda i, kh: (ctx_blk(i), vcol + kh)),
            pl.BlockSpec((LAT_SEQ, HEAD_DIM), lambda i, kh: (lat_blk0 + req(i), kcol + kh)),
            pl.BlockSpec((LAT_SEQ, HEAD_DIM), lambda i, kh: (lat_blk0 + req(i), vcol + kh)),
            pl.BlockSpec((1, 1, CTX_SEQ, HEAD_DIM), lambda i, kh: (req(i), layer, 0, kh)),
            pl.BlockSpec((1, 1, CTX_SEQ, HEAD_DIM), lambda i, kh: (req(i), layer, 0, kh)),
        ],
        out_specs=pl.BlockSpec((ATT_TILE, wq), lambda i, kh: (i, kh)),
        compiler_params=_cparams("arbitrary", "arbitrary"),
        name="gqa_attention",
    )(proj_b, proj_b, proj_b, proj_b, proj_b, cache_k, cache_v)


def _outproj_kernel(x_ref, ya_ref, yb_ref, yc_ref, wo_ref, gate_ref, shift_ref, scale_ref, g2_ref,
                    wr_ref, br_ref, xo_ref, h_ref, lg_ref, *, n_ctx):
    crow = _cond_row(pl.program_id(0), OUTPROJ_TM, n_ctx)
    mix = (jnp.dot(ya_ref[...], wo_ref[0, 0:POOL_WIDTH, :], preferred_element_type=F32)
           + jnp.dot(yb_ref[...], wo_ref[0, POOL_WIDTH:POOL_WIDTH + NA_WIDTH, :], preferred_element_type=F32)
           + jnp.dot(yc_ref[...], wo_ref[0, POOL_WIDTH + NA_WIDTH:, :], preferred_element_type=F32))
    x = x_ref[...] + gate_ref[0, 0, pl.ds(crow, 1), :] * mix
    xo_ref[...] = x
    y = x * lax.rsqrt(jnp.mean(x * x, axis=-1, keepdims=True) + EPS) * g2_ref[...]
    h = y * (1 + scale_ref[0, 0, pl.ds(crow, 1), :]) + shift_ref[0, 0, pl.ds(crow, 1), :]
    h_ref[...] = h
    lg_ref[...] = jnp.dot(h.astype(BF16), wr_ref[...], preferred_element_type=F32) + br_ref[...]


def _outproj(x, ya, yb, yc, w_out_b, mod, layer, g2, w_router_b, b_router_p, n_ctx):
    n_tok = x.shape[0]
    tm = OUTPROJ_TM

    def mod_spec(chunk):
        return pl.BlockSpec((1, 1, COND_ROWS, D_MODEL), lambda i: (layer, chunk, 0, 0))

    return pl.pallas_call(
        functools.partial(_outproj_kernel, n_ctx=n_ctx),
        out_shape=(jax.ShapeDtypeStruct((n_tok, D_MODEL), F32),
                   jax.ShapeDtypeStruct((n_tok, D_MODEL), F32),
                   jax.ShapeDtypeStruct((n_tok, LANES), F32)),
        grid=(n_tok // tm,),
        in_specs=[
            pl.BlockSpec((tm, D_MODEL), lambda i: (i, 0)),
            pl.BlockSpec((tm, POOL_WIDTH), lambda i: (i, 0)),
            pl.BlockSpec((tm, NA_WIDTH), lambda i: (i, 0)),
            pl.BlockSpec((tm, GQA_WIDTH), lambda i: (i, 0)),
            pl.BlockSpec((1, D_MODEL, D_MODEL), lambda i: (layer, 0, 0)),
            mod_spec(2), mod_spec(3), mod_spec(4),
            pl.BlockSpec((1, D_MODEL), lambda i: (0, 0)),
            pl.BlockSpec((D_MODEL, LANES), lambda i: (0, 0)),
            pl.BlockSpec((1, LANES), lambda i: (0, 0)),
        ],
        out_specs=(pl.BlockSpec((tm, D_MODEL), lambda i: (i, 0)),
                   pl.BlockSpec((tm, D_MODEL), lambda i: (i, 0)),
                   pl.BlockSpec((tm, LANES), lambda i: (i, 0))),
        compiler_params=_cparams("arbitrary"),
        name="outproj_norm_router",
    )(x, ya, yb, yc, w_out_b, mod, mod, mod, g2, w_router_b, b_router_p)


def _route_kernel(lg_ref, code_ref, gate_ref, cnt_ref, carry_scr):
    i = pl.program_id(0)

    @pl.when(i == 0)
    def _():
        carry_scr[...] = jnp.zeros_like(carry_scr)

    tm = lg_ref.shape[0]
    l = lg_ref[...]
    lane = lax.broadcasted_iota(I32, (tm, LANES), 1)
    sel = jnp.zeros((tm, LANES), F32)
    vals, idxs = [], []
    for _ in range(TOP_K):
        m = jnp.max(l, axis=-1, keepdims=True)
        ik = jnp.min(jnp.where(l == m, lane, LANES), axis=-1, keepdims=True)
        hit = lane == ik
        sel = jnp.where(hit, 1.0, sel)
        l = jnp.where(hit, -jnp.inf, l)
        vals.append(m)
        idxs.append(ik)
    exps = [jnp.exp(v - vals[0]) for v in vals]
    tot = exps[0] + exps[1] + exps[2] + exps[3]
    r = lax.broadcasted_iota(I32, (tm, tm), 0)
    c = lax.broadcasted_iota(I32, (tm, tm), 1)
    tri = jnp.where(c < r, 1.0, 0.0).astype(BF16)
    before = jnp.dot(tri, sel.astype(BF16), preferred_element_type=F32) + carry_scr[...]
    code_out = jnp.zeros((tm, LANES), I32)
    gate_out = jnp.zeros((tm, LANES), F32)
    for k in range(TOP_K):
        pk = jnp.sum(jnp.where(lane == idxs[k], before, 0.0), axis=-1, keepdims=True)
        code_out = jnp.where(lane == k, idxs[k] * SLOT_BASE + pk.astype(I32), code_out)
        gate_out = jnp.where(lane == k, exps[k] / tot, gate_out)
    code_ref[...] = code_out
    gate_ref[...] = gate_out
    carry_scr[...] = carry_scr[...] + jnp.sum(sel, axis=0, keepdims=True)
    cnt_ref[...] = jnp.broadcast_to(carry_scr[...], cnt_ref.shape)


def _route(logits):
    n_tok = logits.shape[0]
    assert n_tok <= SLOT_BASE
    tm = ROUTE_TM
    tile = pl.BlockSpec((tm, LANES), lambda i: (i, 0))
    return pl.pallas_call(
        _route_kernel,
        out_shape=(jax.ShapeDtypeStruct((n_tok, LANES), I32),
                   jax.ShapeDtypeStruct((n_tok, LANES), F32),
                   jax.ShapeDtypeStruct((8, LANES), F32)),
        grid=(n_tok // tm,),
        in_specs=[tile],
        out_specs=(tile, tile, pl.BlockSpec((8, LANES), lambda i: (0, 0))),
        scratch_shapes=[pltpu.VMEM((1, LANES), F32)],
        compiler_params=_cparams("arbitrary"),
        name="route_topk",
    )(logits)


ZERO_ROWS = 64


def _dispatch_kernel(cnt_ref, start_ref, nused_ref, row_ref, h_ref, xs_hbm, zero_scr, sem, *, n_blocks):
    i = pl.program_id(0)
    last = pl.num_programs(0) - 1

    def row_copy(src_row, dst_row):
        return pltpu.make_async_copy(h_ref.at[pl.ds(src_row, 1)], xs_hbm.at[pl.ds(dst_row, 1)], sem)

    def zero_row_copy(dst_row):
        return pltpu.make_async_copy(zero_scr.at[pl.ds(0, 1)], xs_hbm.at[pl.ds(dst_row, 1)], sem)

    def zero_chunk_copy(dst_row):
        return pltpu.make_async_copy(zero_scr, xs_hbm.at[pl.ds(dst_row, ZERO_ROWS)], sem)

    def issue(tt, carry):
        t0 = pl.multiple_of(tt * DMA_UNROLL, DMA_UNROLL)
        for u in range(DMA_UNROLL):
            for k in range(TOP_K):
                dst = row_ref[0, 0, (t0 + u) * TOP_K + k]
                pltpu.make_async_copy(h_ref.at[pl.ds(t0, DMA_UNROLL)].at[pl.ds(u, 1)],
                                      xs_hbm.at[pl.ds(dst, 1)], sem).start(priority=k % 2)
        return carry

    lax.fori_loop(0, TOKEN_TILE // DMA_UNROLL, issue, 0)

    def drain(t, carry):
        for k in range(TOP_K):
            row_copy(0, 0).wait()
        return carry

    lax.fori_loop(0, TOKEN_TILE, drain, 0, unroll=DMA_UNROLL)

    @pl.when(i == last)
    def _():
        zero_scr[...] = jnp.zeros_like(zero_scr)

        def per_expert(e, carry):
            cnt = cnt_ref[e]
            padded = (cnt + MOE_BLOCK - 1) // MOE_BLOCK * MOE_BLOCK
            base = start_ref[e]

            def z_issue(p, c):
                zero_row_copy(base + p).start()
                return c

            def z_drain(p, c):
                zero_row_copy(0).wait()
                return c

            lax.fori_loop(cnt, padded, z_issue, 0)
            lax.fori_loop(cnt, padded, z_drain, 0)
            return carry

        lax.fori_loop(0, N_EXPERTS, per_expert, 0)

        chunks = MOE_BLOCK // ZERO_ROWS

        def t_issue(c, carry):
            zero_chunk_copy(c * ZERO_ROWS).start()
            return carry

        def t_drain(c, carry):
            zero_chunk_copy(0).wait()
            return carry

        lax.fori_loop(nused_ref[0] * chunks, n_blocks * chunks, t_issue, 0)
        lax.fori_loop(nused_ref[0] * chunks, n_blocks * chunks, t_drain, 0)


def _dispatch(counts, starts, nused, rows, h, n_blocks):
    n_tok = h.shape[0]
    nt = n_tok // TOKEN_TILE
    grid_spec = pltpu.PrefetchScalarGridSpec(
        num_scalar_prefetch=3,
        grid=(nt,),
        in_specs=[
            pl.BlockSpec((1, 1, TOKEN_TILE * TOP_K), lambda i, *_: (i, 0, 0), memory_space=pltpu.SMEM),
            pl.BlockSpec((TOKEN_TILE, D_MODEL), lambda i, *_: (i, 0)),
        ],
        out_specs=pl.BlockSpec(memory_space=pl.ANY),
        scratch_shapes=[pltpu.VMEM((ZERO_ROWS, D_MODEL), F32), pltpu.SemaphoreType.DMA],
    )
    return pl.pallas_call(
        functools.partial(_dispatch_kernel, n_blocks=n_blocks),
        out_shape=jax.ShapeDtypeStruct((n_blocks * MOE_BLOCK, D_MODEL), F32),
        grid_spec=grid_spec,
        compiler_params=_cparams("arbitrary"),
        name="moe_dispatch",
    )(counts, starts, nused, rows, h)


def _expert_kernel(blk_e_ref, nused_ref, xs_ref, wgu_ref, bgu_ref, wd_ref, bd_ref, ys_ref, wgu_scr, wd_scr):
    b = pl.program_id(0)
    e = blk_e_ref[b]
    prev = blk_e_ref[jnp.maximum(b - 1, 0)]

    @pl.when((b == 0) | (e != prev))
    def _():
        wgu_scr[...] = wgu_ref[0, 0].astype(BF16)
        wd_scr[...] = wd_ref[0, 0].astype(BF16)

    @pl.when(b < nused_ref[0])
    def _():
        gu = jnp.dot(xs_ref[...].astype(BF16), wgu_scr[...], preferred_element_type=F32) + bgu_ref[0, 0]
        gate = jnp.minimum(gu[:, :D_FF], SWIGLU_LIMIT)
        up = jnp.clip(gu[:, D_FF:], -SWIGLU_LIMIT, SWIGLU_LIMIT)
        act = (up + 1) * (gate * jax.nn.sigmoid(SWIGLU_ALPHA * gate))
        ys_ref[...] = jnp.dot(act.astype(BF16), wd_scr[...], preferred_element_type=F32) + bd_ref[0, 0]

    @pl.when(b >= nused_ref[0])
    def _():
        ys_ref[...] = jnp.zeros_like(ys_ref)


def _experts(blk_e, nused, xs, w_gate_up, b_gate_up, w_down, b_down, layer):
    n_blocks = blk_e.shape[0]
    depth = w_gate_up.shape[0]
    grid_spec = pltpu.PrefetchScalarGridSpec(
        num_scalar_prefetch=2,
        grid=(n_blocks,),
        in_specs=[
            pl.BlockSpec((MOE_BLOCK, D_MODEL), lambda b, be, nu: (jnp.minimum(b, nu[0] - 1), 0)),
            pl.BlockSpec((1, 1, D_MODEL, 2 * D_FF), lambda b, be, nu: (layer, be[b], 0, 0)),
            pl.BlockSpec((1, 1, 1, 2 * D_FF), lambda b, be, nu: (layer, be[b], 0, 0)),
            pl.BlockSpec((1, 1, D_FF, D_MODEL), lambda b, be, nu: (layer, be[b], 0, 0)),
            pl.BlockSpec((1, 1, 1, D_MODEL), lambda b, be, nu: (layer, be[b], 0, 0)),
        ],
        out_specs=pl.BlockSpec((MOE_BLOCK, D_MODEL), lambda b, be, nu: (b, 0)),
        scratch_shapes=[pltpu.VMEM((D_MODEL, 2 * D_FF), BF16), pltpu.VMEM((D_FF, D_MODEL), BF16)],
    )
    return pl.pallas_call(
        _expert_kernel,
        out_shape=jax.ShapeDtypeStruct(xs.shape, F32),
        grid_spec=grid_spec,
        compiler_params=_cparams("arbitrary"),
        name="moe_experts",
    )(blk_e, nused, xs, w_gate_up, b_gate_up.reshape(depth, N_EXPERTS, 1, 2 * D_FF),
      w_down, b_down.reshape(depth, N_EXPERTS, 1, D_MODEL))


def _combine_kernel(row_ref, next_row_ref, gate_ref, x_ref, mgate_ref, gfin_ref, ys_hbm, o_ref,
                    rows_scr, sems, *, n_ctx, final_norm):
    i = pl.program_id(0)
    n_steps = pl.num_programs(0)
    crow = _cond_row(i, TOKEN_TILE, n_ctx)
    slot = i % 2

    def row_copy(buf, src_row, k, t):
        return pltpu.make_async_copy(ys_hbm.at[pl.ds(src_row, 1)], rows_scr.at[buf, k, pl.ds(t, 1)],
                                     sems.at[buf])

    def gather(rows, buf):
        def issue(tt, carry):
            t0 = pl.multiple_of(tt * DMA_UNROLL, DMA_UNROLL)
            for u in range(DMA_UNROLL):
                for k in range(TOP_K):
                    src = rows[0, 0, (t0 + u) * TOP_K + k]
                    pltpu.make_async_copy(ys_hbm.at[pl.ds(src, 1)],
                                          rows_scr.at[buf, k, pl.ds(t0, DMA_UNROLL)].at[pl.ds(u, 1)],
                                          sems.at[buf]).start(priority=k % 2)
            return carry

        lax.fori_loop(0, TOKEN_TILE // DMA_UNROLL, issue, 0)

    @pl.when(i == 0)
    def _():
        gather(row_ref, 0)

    @pl.when(i + 1 < n_steps)
    def _():
        gather(next_row_ref, 1 - slot)

    def drain(t, carry):
        for k in range(TOP_K):
            row_copy(slot, 0, 0, 0).wait()
        return carry

    lax.fori_loop(0, TOKEN_TILE, drain, 0, unroll=DMA_UNROLL)

    g = gate_ref[...]
    y = g[:, 0:1] * rows_scr[slot, 0]
    for k in range(1, TOP_K):
        y = y + g[:, k:k + 1] * rows_scr[slot, k]
    x = x_ref[...] + mgate_ref[0, 0, pl.ds(crow, 1), :] * y
    if final_norm:
        x = x * lax.rsqrt(jnp.mean(x * x, axis=-1, keepdims=True) + EPS) * gfin_ref[...]
    o_ref[...] = x


def _combine(rows, gates, x, mod, layer, ys, g_final, n_ctx, final_norm):
    n_tok = x.shape[0]
    nt = n_tok // TOKEN_TILE
    row_block = (1, 1, TOKEN_TILE * TOP_K)
    return pl.pallas_call(
        functools.partial(_combine_kernel, n_ctx=n_ctx, final_norm=final_norm),
        out_shape=jax.ShapeDtypeStruct((n_tok, D_MODEL), F32),
        grid=(nt,),
        in_specs=[
            pl.BlockSpec(row_block, lambda i: (i, 0, 0), memory_space=pltpu.SMEM),
            pl.BlockSpec(row_block, lambda i: (jnp.minimum(i + 1, nt - 1), 0, 0), memory_space=pltpu.SMEM),
            pl.BlockSpec((TOKEN_TILE, LANES), lambda i: (i, 0)),
            pl.BlockSpec((TOKEN_TILE, D_MODEL), lambda i: (i, 0)),
            pl.BlockSpec((1, 1, COND_ROWS, D_MODEL), lambda i: (layer, 5, 0, 0)),
            pl.BlockSpec((1, D_MODEL), lambda i: (0, 0)),
            pl.BlockSpec(memory_space=pl.ANY),
        ],
        out_specs=pl.BlockSpec((TOKEN_TILE, D_MODEL), lambda i: (i, 0)),
        scratch_shapes=[pltpu.VMEM((2, TOP_K, TOKEN_TILE, D_MODEL), F32), pltpu.SemaphoreType.DMA((2,))],
        compiler_params=_cparams("arbitrary"),
        name="moe_combine",
    )(rows, rows, gates, x, mod, g_final, ys)


def _rope_tables():
    t = np.arange(LAT_SEQ)
    row = (t // GRID_W).astype(np.float32)
    col = (t % GRID_W).astype(np.float32)
    half = HEAD_DIM // 2
    inv = jnp.asarray(ROPE_THETA, F32) ** (-jnp.arange(0, half, 2, dtype=F32) / half)
    ang = jnp.concatenate([row[:, None] * inv, col[:, None] * inv], axis=-1)
    cos, sin = jnp.cos(ang), jnp.sin(ang)
    return jnp.concatenate([cos, cos], axis=-1), jnp.concatenate([-sin, sin], axis=-1)


def _slot_rows(codes, starts):
    expert = lax.shift_right_logical(codes, SLOT_SHIFT)
    base = jnp.sum(jnp.where(expert[..., None] == jnp.arange(N_EXPERTS, dtype=I32), starts, 0), axis=-1)
    return base + (codes & (SLOT_BASE - 1))


def _routing_plan(counts_f, n_blocks):
    counts = counts_f[0, :N_EXPERTS].astype(I32)
    padded = (counts + MOE_BLOCK - 1) // MOE_BLOCK * MOE_BLOCK
    pend = jnp.cumsum(padded)
    starts = pend - padded
    nused = pend[-1] // MOE_BLOCK
    blk = jnp.minimum(jnp.arange(n_blocks, dtype=I32), nused - 1)
    blk_e = jnp.sum((pend[None, :] <= (blk * MOE_BLOCK)[:, None]).astype(I32), axis=1)
    blk_e = jnp.minimum(blk_e, N_EXPERTS - 1)
    return counts, starts.astype(I32), nused.reshape(1).astype(I32), blk_e


def _forward(x_prompt, x_sample, cache_na_k, cache_na_v, cache_gqa_k, cache_gqa_v, c, c_ctx,
             w_ada, b_ada, g_norm1, g_norm2, w_in, w_pool, pool_scale, na_rel_bias, g_q, g_k,
             w_out, w_router, b_router, w_gate_up, b_gate_up, w_down, b_down, g_final):
    depth = w_in.shape[0]
    nb_ctx, nb_lat = x_prompt.shape[0], x_sample.shape[0]
    n_ctx, n_lat = nb_ctx * CTX_SEQ, nb_lat * LAT_SEQ
    n_tok = n_ctx + n_lat
    assert x_prompt.shape[1:] == (CTX_SEQ, D_MODEL) and x_sample.shape[1:] == (LAT_SEQ, D_MODEL)
    assert n_ctx % LAT_SEQ == 0 and 1 + nb_lat <= COND_ROWS
    n_blocks = n_tok * TOP_K // MOE_BLOCK + N_EXPERTS

    x = jnp.concatenate([x_prompt.reshape(n_ctx, D_MODEL), x_sample.reshape(n_lat, D_MODEL)], axis=0)
    cond = jnp.zeros((COND_ROWS, D_MODEL), F32).at[0].set(c_ctx).at[1:1 + nb_lat].set(c)
    mod = _ada_mod(cond, w_ada, b_ada)

    cos2, sin2 = _rope_tables()
    w_in_b = w_in.astype(BF16)
    w_out_b = w_out.astype(BF16)
    w_pool_b = w_pool.astype(BF16)
    w_router_b = jnp.pad(w_router, ((0, 0), (0, 0), (0, LANES - N_EXPERTS))).astype(BF16)
    b_router_p = jnp.pad(b_router, ((0, 0), (0, LANES - N_EXPERTS)), constant_values=NEG)
    cna_k = cache_na_k.reshape(nb_lat, depth, CTX_SEQ, NA_WIDTH).astype(BF16)
    cna_v = cache_na_v.reshape(nb_lat, depth, CTX_SEQ, NA_WIDTH).astype(BF16)
    cgq_k = cache_gqa_k.reshape(nb_lat, depth, CTX_SEQ, GQA_KV_WIDTH).astype(BF16)
    cgq_v = cache_gqa_v.reshape(nb_lat, depth, CTX_SEQ, GQA_KV_WIDTH).astype(BF16)

    new_kv = []
    for l in range(depth):
        proj_f, proj_b = _inproj(x, mod, l, g_norm1[l][None], w_in_b, g_q[l][None], g_k[l][None],
                                 cos2, sin2, n_ctx)
        new_kv.append(proj_f[:n_ctx])
        ya = _pool(proj_f, w_pool_b[l], pool_scale[l][None], n_ctx)
        yb = _na_attention(proj_b, cna_k, cna_v, _na_bias_tables(na_rel_bias[l]), l, n_ctx)
        yc = _gqa_attention(proj_b, cgq_k, cgq_v, l, n_ctx)
        x, h, logits = _outproj(x, ya, yb, yc, w_out_b, mod, l, g_norm2[l][None],
                                w_router_b[l], b_router_p[l][None], n_ctx)
        codes, gates, counts_f = _route(logits)
        counts, starts, nused, blk_e = _routing_plan(counts_f, n_blocks)
        rows = _slot_rows(codes[:, :TOP_K], starts).reshape(n_tok // TOKEN_TILE, 1, TOKEN_TILE * TOP_K)
        xs = _dispatch(counts, starts, nused, rows, h, n_blocks)
        ys = _experts(blk_e, nused, xs, w_gate_up, b_gate_up, w_down, b_down, l)
        x = _combine(rows, gates, x, mod, l, ys, g_final[None], n_ctx, final_norm=(l == depth - 1))

    y = x
    y_prompt = y[:n_ctx].reshape(nb_ctx, CTX_SEQ, D_MODEL)
    y_sample = y[n_ctx:].reshape(nb_lat, LAT_SEQ, D_MODEL)

    def stack(col, width):
        per_layer = [p[:, col:col + width].reshape(nb_ctx, CTX_SEQ, width // HEAD_DIM, HEAD_DIM) for p in new_kv]
        return jnp.stack(per_layer, axis=1)

    return (y_prompt, y_sample, stack(COL_KN, NA_WIDTH), stack(COL_VN, NA_WIDTH),
            stack(COL_KG, GQA_KV_WIDTH), stack(COL_VG, GQA_KV_WIDTH))


def kernel(x_prompt, x_sample, cache_na_k, cache_na_v, cache_gqa_k, cache_gqa_v, c, c_ctx, w_ada, b_ada,
           g_norm1, g_norm2, w_in, w_pool, pool_scale, na_rel_bias, g_q, g_k, w_out, w_router, b_router,
           w_gate_up, b_gate_up, w_down, b_down, g_final):
    return _forward(x_prompt, x_sample, cache_na_k, cache_na_v, cache_gqa_k, cache_gqa_v, c, c_ctx,
                    w_ada, b_ada, g_norm1, g_norm2, w_in, w_pool, pool_scale, na_rel_bias, g_q, g_k,
                    w_out, w_router, b_router, w_gate_up, b_gate_up, w_down, b_down, g_final)
```

```python
import functools
import math

import numpy as np
import jax
import jax.numpy as jnp
from jax import lax
from jax.experimental import pallas as pl
from jax.experimental.pallas import tpu as pltpu

F32 = jnp.float32
BF16 = jnp.bfloat16
I32 = jnp.int32

D_MODEL = 2048
HEAD_DIM = 128
CTX_SEQ = 256
LAT_SEQ = 2048
GRID_W = 64
GRID_ROWS = LAT_SEQ // GRID_W
POOL_WINDOWS = (2, 4, 8, 16)
POOL_GROUP = 128
POOL_WIDTH = 512
NA_HEADS = 4
NA_WIDTH = 512
NA_ROWS = 8
NA_COLS = 16
GQA_HEADS = 8
GQA_KV_HEADS = 2
GQA_GROUP = GQA_HEADS // GQA_KV_HEADS
GQA_WIDTH = 1024
GQA_KV_WIDTH = 256
IN_WIDTH = 3584
N_EXPERTS = 32
TOP_K = 4
D_FF = 512
SWIGLU_LIMIT = 7.0
SWIGLU_ALPHA = 1.702
ROPE_THETA = 10000.0
EPS = 1e-6
NEG = -1e30

COL_U, COL_QN, COL_KN, COL_VN, COL_QG, COL_KG, COL_VG = 0, 512, 1024, 1536, 2048, 3072, 3328
ACT_QN, ACT_KN, ACT_VN, ACT_QG, ACT_KG, ACT_VG = (c - POOL_WIDTH for c in (COL_QN, COL_KN, COL_VN, COL_QG, COL_KG, COL_VG))
ACT_WIDTH = IN_WIDTH - POOL_WIDTH
KVF_KN, KVF_VN, KVF_KG, KVF_VG, KVF_WIDTH = 0, 512, 1024, 1280, 1536

LANES = 128
COND_ROWS = 16
ATT_TILE = 256
NA_TILE_ROWS = ATT_TILE // GRID_W
NA_WIN_ROWS = NA_ROWS + NA_TILE_ROWS - 1
NA_WIN = NA_WIN_ROWS * GRID_W
INPROJ_TM = 512
OUTPROJ_TM = 512
ROUTE_TM = 512
MOE_BLOCK = 512
TOKEN_TILE = 256
SLOT_SHIFT = 16
SLOT_BASE = 1 << SLOT_SHIFT
DMA_UNROLL = 8
VMEM_LIMIT = 56 * 1024 * 1024


def _cparams(*sem):
    return pltpu.CompilerParams(dimension_semantics=sem, vmem_limit_bytes=VMEM_LIMIT)


def _cond_row(tile, tile_rows, n_ctx):
    row0 = tile * tile_rows
    return jnp.where(row0 < n_ctx, 0, 1 + (row0 - n_ctx) // LAT_SEQ)


def _head(h):
    return slice(h * HEAD_DIM, (h + 1) * HEAD_DIM)


U32 = jnp.uint32
PACKED_WIDTH = D_MODEL // 2


def _pack_bf16_pairs(x):
    half = x.shape[1] // 2
    hi = pltpu.bitcast(x[:, :half].astype(BF16).astype(F32), U32)
    lo = pltpu.bitcast(x[:, half:].astype(BF16).astype(F32), U32)
    return hi | (lo >> 16)


def _unpack_bf16_pairs(p):
    first = pltpu.bitcast(p & jnp.uint32(0xFFFF0000), F32).astype(BF16)
    second = pltpu.bitcast(p << 16, F32).astype(BF16)
    return first, second


def _ada_kernel(c_ref, w_ref, b_ref, o_ref):
    c = c_ref[...]
    s = (c * jax.nn.sigmoid(c)).astype(BF16)
    o_ref[0, 0] = jnp.dot(s, w_ref[0].astype(BF16), preferred_element_type=F32) + b_ref[0]


def _ada_mod(cond, w_ada, b_ada):
    depth = w_ada.shape[0]
    tn = 1024
    per = D_MODEL // tn
    return pl.pallas_call(
        _ada_kernel,
        out_shape=jax.ShapeDtypeStruct((depth, 6, COND_ROWS, D_MODEL), F32),
        grid=(depth, 6 * per),
        in_specs=[
            pl.BlockSpec((COND_ROWS, D_MODEL), lambda l, j: (0, 0)),
            pl.BlockSpec((1, D_MODEL, tn), lambda l, j: (l, 0, j)),
            pl.BlockSpec((1, 1, tn), lambda l, j: (l, 0, j)),
        ],
        out_specs=pl.BlockSpec((1, 1, COND_ROWS, tn), lambda l, j: (l, j // per, 0, j % per)),
        compiler_params=_cparams("arbitrary", "arbitrary"),
        name="ada_mod",
    )(cond, w_ada, b_ada.reshape(depth, 1, 6 * D_MODEL))


def _head_rms(x, g):
    return x * lax.rsqrt(jnp.mean(x * x, axis=-1, keepdims=True) + EPS) * g


NORM_CHUNKS = 8


def _inproj_kernel(x0_ref, xn_ref, shift_ref, scale_ref, g1_ref, w_ref, gq_ref, gk_ref, cos_ref, sin_ref,
                   u_ref, kv_ref, act_ref, h_even, h_odd, *, n_ctx):
    i = pl.program_id(0)
    nxt = jnp.minimum(i + 1, pl.num_programs(0) - 1)
    is_lat = i * INPROJ_TM >= n_ctx
    cos = jnp.where(is_lat, cos_ref[...], 1.0)
    sin = jnp.where(is_lat, sin_ref[...], 0.0)

    def normalize(x_ref, tile, h_ref, rows):
        crow = _cond_row(tile, INPROJ_TM, n_ctx)
        x = x_ref[rows, :]
        y = x * lax.rsqrt(jnp.mean(x * x, axis=-1, keepdims=True) + EPS) * g1_ref[...]
        h_ref[rows, :] = (y * (1 + scale_ref[0, 0, pl.ds(crow, 1), :])
                          + shift_ref[0, 0, pl.ds(crow, 1), :]).astype(BF16)

    def normed_rope(xh, g_ref):
        xh = _head_rms(xh, g_ref[...])
        return xh * cos + pltpu.roll(xh, HEAD_DIM // 2, 1) * sin

    def project(h_ref, h_next_ref):
        chunk = INPROJ_TM // NORM_CHUNKS
        steps = iter(range(NORM_CHUNKS))

        def proj(col, width):
            acc = jnp.dot(h_ref[...], w_ref[0, :, col:col + width], preferred_element_type=F32)
            c = next(steps)
            normalize(xn_ref, nxt, h_next_ref, slice(c * chunk, (c + 1) * chunk))
            return acc

        for half in range(GQA_KV_HEADS):
            q = proj(COL_QG + half * GQA_GROUP * HEAD_DIM, GQA_GROUP * HEAD_DIM)
            for g in range(GQA_GROUP):
                col = ACT_QG + (half * GQA_GROUP + g) * HEAD_DIM
                act_ref[:, col:col + HEAD_DIM] = normed_rope(q[:, _head(g)], gq_ref).astype(BF16)
        kg = proj(COL_KG, GQA_KV_WIDTH)
        for kh in range(GQA_KV_HEADS):
            v = normed_rope(kg[:, _head(kh)], gk_ref)
            kv_ref[:, KVF_KG + kh * HEAD_DIM:KVF_KG + (kh + 1) * HEAD_DIM] = v
            act_ref[:, ACT_KG + kh * HEAD_DIM:ACT_KG + (kh + 1) * HEAD_DIM] = v.astype(BF16)
        vg = proj(COL_VG, GQA_KV_WIDTH)
        kv_ref[:, KVF_VG:KVF_VG + GQA_KV_WIDTH] = vg
        act_ref[:, ACT_VG:ACT_VG + GQA_KV_WIDTH] = vg.astype(BF16)
        for col, kv_col, act_col in ((COL_KN, KVF_KN, ACT_KN), (COL_VN, KVF_VN, ACT_VN)):
            v = proj(col, NA_WIDTH)
            kv_ref[:, kv_col:kv_col + NA_WIDTH] = v
            act_ref[:, act_col:act_col + NA_WIDTH] = v.astype(BF16)
        act_ref[:, ACT_QN:ACT_QN + NA_WIDTH] = proj(COL_QN, NA_WIDTH).astype(BF16)
        u_ref[...] = proj(COL_U, POOL_WIDTH)

    @pl.when(i == 0)
    def _():
        normalize(x0_ref, 0, h_even, slice(None))

    @pl.when(i % 2 == 0)
    def _():
        project(h_even, h_odd)

    @pl.when(i % 2 == 1)
    def _():
        project(h_odd, h_even)


def _inproj(x, mod, layer, g1, w_in_b, g_q, g_k, cos2, sin2, n_ctx):
    n_tok = x.shape[0]
    tm = INPROJ_TM
    lat_tiles = LAT_SEQ // tm
    ctx_tiles = n_ctx // tm

    def rope_map(i):
        return (jnp.maximum(i - ctx_tiles, 0) % lat_tiles, 0)

    def rows(width):
        return pl.BlockSpec((tm, width), lambda i: (i, 0))

    return pl.pallas_call(
        functools.partial(_inproj_kernel, n_ctx=n_ctx),
        out_shape=(jax.ShapeDtypeStruct((n_tok, POOL_WIDTH), F32),
                   jax.ShapeDtypeStruct((n_tok, KVF_WIDTH), F32),
                   jax.ShapeDtypeStruct((n_tok, ACT_WIDTH), BF16)),
        grid=(n_tok // tm,),
        in_specs=[
            pl.BlockSpec((tm, D_MODEL), lambda i: (0, 0)),
            pl.BlockSpec((tm, D_MODEL), lambda i: (jnp.minimum(i + 1, n_tok // tm - 1), 0)),
            pl.BlockSpec((1, 1, COND_ROWS, D_MODEL), lambda i: (layer, 0, 0, 0)),
            pl.BlockSpec((1, 1, COND_ROWS, D_MODEL), lambda i: (layer, 1, 0, 0)),
            pl.BlockSpec((1, D_MODEL), lambda i: (0, 0)),
            pl.BlockSpec((1, D_MODEL, IN_WIDTH), lambda i: (layer, 0, 0), pipeline_mode=pl.Buffered(1)),
            pl.BlockSpec((1, HEAD_DIM), lambda i: (0, 0)),
            pl.BlockSpec((1, HEAD_DIM), lambda i: (0, 0)),
            pl.BlockSpec((tm, HEAD_DIM), rope_map),
            pl.BlockSpec((tm, HEAD_DIM), rope_map),
        ],
        out_specs=(rows(POOL_WIDTH), rows(KVF_WIDTH), rows(ACT_WIDTH)),
        scratch_shapes=[pltpu.VMEM((tm, D_MODEL), BF16), pltpu.VMEM((tm, D_MODEL), BF16)],
        compiler_params=_cparams("arbitrary"),
        name="norm_inproj",
    )(x, x, mod, mod, g1, w_in_b, g_q, g_k, cos2, sin2)


POOL_HALO = 8


def _pool_kernel(up_ref, uc_ref, un_ref, wp_ref, ps_ref, o_ref, pad_scr, *, n_ctx):
    i = pl.program_id(0)
    row0 = i * ATT_TILE
    is_ctx = row0 < n_ctx
    t0 = jnp.where(is_ctx, 0, (row0 - n_ctx) % LAT_SEQ)
    seq = jnp.where(is_ctx, CTX_SEQ, LAT_SEQ)
    first = t0 == 0
    last = t0 + ATT_TILE == seq
    zeros = jnp.zeros((POOL_HALO, POOL_WIDTH), F32)
    pad_scr[0:POOL_HALO, :] = jnp.where(first, zeros, up_ref[ATT_TILE - POOL_HALO:ATT_TILE, :])
    pad_scr[POOL_HALO:POOL_HALO + ATT_TILE, :] = uc_ref[...]
    pad_scr[POOL_HALO + ATT_TILE:, :] = jnp.where(last, zeros, un_ref[0:POOL_HALO, :])
    t = t0 + lax.broadcasted_iota(I32, (ATT_TILE, 1), 0)
    for g, w in enumerate(POOL_WINDOWS):
        cols = slice(g * POOL_GROUP, (g + 1) * POOL_GROUP)
        acc = pad_scr[pl.ds(POOL_HALO - w // 2, ATT_TILE), cols]
        for d in range(-w // 2 + 1, w // 2):
            acc = acc + pad_scr[pl.ds(POOL_HALO + d, ATT_TILE), cols]
        cnt = (jnp.minimum(t + w // 2, seq) - jnp.maximum(t - w // 2, 0)).astype(F32)
        pooled = acc / cnt - uc_ref[:, cols]
        y = jnp.dot(pooled.astype(BF16), wp_ref[g], preferred_element_type=F32)
        o_ref[:, cols] = (y * ps_ref[:, cols]).astype(BF16)


def _pool(proj_f, w_pool_b, pool_scale, n_ctx):
    n_tok = proj_f.shape[0]
    nt = n_tok // ATT_TILE
    return pl.pallas_call(
        functools.partial(_pool_kernel, n_ctx=n_ctx),
        out_shape=jax.ShapeDtypeStruct((n_tok, POOL_WIDTH), BF16),
        grid=(nt,),
        in_specs=[
            pl.BlockSpec((ATT_TILE, POOL_WIDTH), lambda i: (jnp.maximum(i - 1, 0), 0)),
            pl.BlockSpec((ATT_TILE, POOL_WIDTH), lambda i: (i, 0)),
            pl.BlockSpec((ATT_TILE, POOL_WIDTH), lambda i: (jnp.minimum(i + 1, nt - 1), 0)),
            pl.BlockSpec((len(POOL_WINDOWS), POOL_GROUP, POOL_GROUP), lambda i: (0, 0, 0)),
            pl.BlockSpec((1, POOL_WIDTH), lambda i: (0, 0)),
        ],
        out_specs=pl.BlockSpec((ATT_TILE, POOL_WIDTH), lambda i: (i, 0)),
        scratch_shapes=[pltpu.VMEM((ATT_TILE + 2 * POOL_HALO, POOL_WIDTH), F32)],
        compiler_params=_cparams("arbitrary"),
        name="pool_mixer",
    )(proj_f, proj_f, proj_f, w_pool_b, pool_scale)


_NT_DIMS = (((1,), (1,)), ((), ()))
ATT_SCALE = HEAD_DIM ** -0.5
EXP2_SCALE = ATT_SCALE * math.log2(math.e)


def _scores(q, k):
    return lax.dot_general(q, k, _NT_DIMS, preferred_element_type=F32)


def _attend_one(q, k, v):
    s = _scores(q, k)
    p = jnp.exp2((s - jnp.max(s, axis=-1, keepdims=True)) * EXP2_SCALE)
    o = jnp.dot(p.astype(BF16), v, preferred_element_type=F32)
    return o / jnp.sum(p, axis=-1, keepdims=True)


def _attend_two(s1, v1, s2, v2):
    m = jnp.maximum(jnp.max(s1, axis=-1, keepdims=True), jnp.max(s2, axis=-1, keepdims=True))
    p1 = jnp.exp2((s1 - m) * EXP2_SCALE)
    p2 = jnp.exp2((s2 - m) * EXP2_SCALE)
    o = (jnp.dot(p1.astype(BF16), v1, preferred_element_type=F32)
         + jnp.dot(p2.astype(BF16), v2, preferred_element_type=F32))
    return o / (jnp.sum(p1, axis=-1, keepdims=True) + jnp.sum(p2, axis=-1, keepdims=True))


def _na_kernel(q_ref, kc_ref, vc_ref, kl_ref, vl_ref, ck_ref, cv_ref, bias_ref, o_ref, *, n_ctx_steps):
    i = pl.program_id(0)

    @pl.when(i < n_ctx_steps)
    def _():
        for h in range(NA_HEADS):
            o = _attend_one(q_ref[:, _head(h)], kc_ref[:, _head(h)], vc_ref[:, _head(h)])
            o_ref[:, _head(h)] = o.astype(BF16)

    @pl.when(i >= n_ctx_steps)
    def _():
        tile = (i - n_ctx_steps) % (LAT_SEQ // ATT_TILE)
        win_row = jnp.clip(tile * NA_TILE_ROWS - NA_ROWS // 2, 0, GRID_ROWS - NA_WIN_ROWS)
        key0 = pl.multiple_of(win_row * GRID_W, GRID_W)
        for h in range(NA_HEADS):
            q = q_ref[:, _head(h)]
            s_lat = _scores(q, kl_ref[pl.ds(key0, NA_WIN), _head(h)]) + bias_ref[0, 0, h]
            s_ctx = _scores(q, ck_ref[0, 0, :, _head(h)])
            o = _attend_two(s_lat, vl_ref[pl.ds(key0, NA_WIN), _head(h)],
                            s_ctx, cv_ref[0, 0, :, _head(h)])
            o_ref[:, _head(h)] = o.astype(BF16)


def _na_bias_tables(rel_bias):
    depth = rel_bias.shape[0]
    half = NA_ROWS // 2
    r0 = np.array([0, half, GRID_ROWS - NA_TILE_ROWS])[:, None, None]
    win = np.clip(r0 - half, 0, GRID_ROWS - NA_WIN_ROWS)
    qr = r0 + np.arange(NA_TILE_ROWS)[None, :, None]
    kr = win + np.arange(NA_WIN_ROWS)[None, None, :]
    rs = np.clip(qr - half, 0, GRID_ROWS - NA_ROWS)
    row_ok = (kr >= rs) & (kr < rs + NA_ROWS)
    dr = np.clip(kr - qr + NA_ROWS - 1, 0, 2 * NA_ROWS - 2)
    qc = np.arange(GRID_W)[:, None]
    kc = np.arange(GRID_W)[None, :]
    cs = np.clip(qc - NA_COLS // 2, 0, GRID_W - NA_COLS)
    col_ok = (kc >= cs) & (kc < cs + NA_COLS)
    dc = np.clip(kc - qc + NA_COLS - 1, 0, 2 * NA_COLS - 2)
    sel_r = (dr[..., None] == np.arange(2 * NA_ROWS - 1)).astype(np.float32)
    sel_c = (dc[..., None] == np.arange(2 * NA_COLS - 1)).astype(np.float32)
    vals = jnp.einsum("sqka,lhab,xyb->lshqxky", sel_r, rel_bias, sel_c, precision=lax.Precision.HIGHEST)
    ok = row_ok[:, :, None, :, None] & col_ok[None, None, :, None, :]
    table = jnp.where(ok[None, :, None], vals / ATT_SCALE, NEG)
    return table.reshape(depth, 3, NA_HEADS, ATT_TILE, NA_WIN).astype(F32)


def _na_attention(proj_b, cache_k, cache_v, bias, layer, n_ctx):
    n_tok = proj_b.shape[0]
    n_ctx_steps = n_ctx // ATT_TILE
    lat_tiles = LAT_SEQ // ATT_TILE
    lat_blk0 = n_ctx // LAT_SEQ
    wq = NA_WIDTH

    def ctx_blk(i):
        return jnp.minimum(i, n_ctx_steps - 1)

    def req(i):
        return jnp.maximum(i - n_ctx_steps, 0) // lat_tiles

    def kind(i):
        tile = jnp.maximum(i - n_ctx_steps, 0) % lat_tiles
        return jnp.where(tile == 0, 0, jnp.where(tile == lat_tiles - 1, 2, 1))

    return pl.pallas_call(
        functools.partial(_na_kernel, n_ctx_steps=n_ctx_steps),
        out_shape=jax.ShapeDtypeStruct((n_tok, NA_WIDTH), BF16),
        grid=(n_tok // ATT_TILE,),
        in_specs=[
            pl.BlockSpec((ATT_TILE, wq), lambda i: (i, ACT_QN // wq)),
            pl.BlockSpec((ATT_TILE, wq), lambda i: (ctx_blk(i), ACT_KN // wq)),
            pl.BlockSpec((ATT_TILE, wq), lambda i: (ctx_blk(i), ACT_VN // wq)),
            pl.BlockSpec((LAT_SEQ, wq), lambda i: (lat_blk0 + req(i), ACT_KN // wq)),
            pl.BlockSpec((LAT_SEQ, wq), lambda i: (lat_blk0 + req(i), ACT_VN // wq)),
            pl.BlockSpec((1, 1, CTX_SEQ, wq), lambda i: (req(i), layer, 0, 0)),
            pl.BlockSpec((1, 1, CTX_SEQ, wq), lambda i: (req(i), layer, 0, 0)),
            pl.BlockSpec((1, 1, NA_HEADS, ATT_TILE, NA_WIN), lambda i: (layer, kind(i), 0, 0, 0)),
        ],
        out_specs=pl.BlockSpec((ATT_TILE, wq), lambda i: (i, 0)),
        compiler_params=_cparams("arbitrary"),
        name="na_attention",
    )(proj_b, proj_b, proj_b, proj_b, proj_b, cache_k, cache_v, bias)


def _gqa_kernel(q_ref, kc_ref, vc_ref, kl_ref, vl_ref, ck_ref, cv_ref, o_ref, *, n_ctx_steps):
    i = pl.program_id(0)

    @pl.when(i < n_ctx_steps)
    def _():
        for g in range(GQA_GROUP):
            o = _attend_one(q_ref[:, _head(g)], kc_ref[...], vc_ref[...])
            o_ref[:, _head(g)] = o.astype(BF16)

    @pl.when(i >= n_ctx_steps)
    def _():
        for g in range(GQA_GROUP):
            q = q_ref[:, _head(g)]
            o = _attend_two(_scores(q, kl_ref[...]), vl_ref[...],
                            _scores(q, ck_ref[0, 0]), cv_ref[0, 0])
            o_ref[:, _head(g)] = o.astype(BF16)


def _gqa_attention(proj_b, cache_k, cache_v, layer, n_ctx):
    n_tok = proj_b.shape[0]
    n_ctx_steps = n_ctx // ATT_TILE
    lat_tiles = LAT_SEQ // ATT_TILE
    lat_blk0 = n_ctx // LAT_SEQ
    wq = GQA_GROUP * HEAD_DIM
    kcol, vcol = ACT_KG // HEAD_DIM, ACT_VG // HEAD_DIM

    def ctx_blk(i):
        return jnp.minimum(i, n_ctx_steps - 1)

    def req(i):
        return jnp.maximum(i - n_ctx_steps, 0) // lat_tiles

    return pl.pallas_call(
        functools.partial(_gqa_kernel, n_ctx_steps=n_ctx_steps),
        out_shape=jax.ShapeDtypeStruct((n_tok, GQA_WIDTH), BF16),
        grid=(n_tok // ATT_TILE, GQA_KV_HEADS),
        in_specs=[
            pl.BlockSpec((ATT_TILE, wq), lambda i, kh: (i, ACT_QG // wq + kh)),
            pl.BlockSpec((ATT_TILE, HEAD_DIM), lambda i, kh: (ctx_blk(i), kcol + kh)),
            pl.BlockSpec((ATT_TILE, HEAD_DIM), lambda i, kh: (ctx_blk(i), vcol + kh)),
            pl.BlockSpec((LAT_SEQ, HEAD_DIM), lambda i, kh: (lat_blk0 + req(i), kcol + kh)),
            pl.BlockSpec((LAT_SEQ, HEAD_DIM), lambda i, kh: (lat_blk0 + req(i), vcol + kh)),
            pl.BlockSpec((1, 1, CTX_SEQ, HEAD_DIM), lambda i, kh: (req(i), layer, 0, kh)),
            pl.BlockSpec((1, 1, CTX_SEQ, HEAD_DIM), lambda i, kh: (req(i), layer, 0, kh)),
        ],
        out_specs=pl.BlockSpec((ATT_TILE, wq), lambda i, kh: (i, kh)),
        compiler_params=_cparams("arbitrary", "arbitrary"),
        name="gqa_attention",
    )(proj_b, proj_b, proj_b, proj_b, proj_b, cache_k, cache_v)


def _outproj_kernel(x_ref, ya_ref, yb_ref, yc_ref, wo_ref, gate_ref, shift_ref, scale_ref, g2_ref,
                    wr_ref, br_ref, xo_ref, h_ref, lg_ref, *, n_ctx):
    crow = _cond_row(pl.program_id(0), OUTPROJ_TM, n_ctx)
    mix = (jnp.dot(ya_ref[...], wo_ref[0, 0:POOL_WIDTH, :], preferred_element_type=F32)
           + jnp.dot(yb_ref[...], wo_ref[0, POOL_WIDTH:POOL_WIDTH + NA_WIDTH, :], preferred_element_type=F32)
           + jnp.dot(yc_ref[...], wo_ref[0, POOL_WIDTH + NA_WIDTH:, :], preferred_element_type=F32))
    x = x_ref[...] + gate_ref[0, 0, pl.ds(crow, 1), :] * mix
    xo_ref[...] = x
    y = x * lax.rsqrt(jnp.mean(x * x, axis=-1, keepdims=True) + EPS) * g2_ref[...]
    h = y * (1 + scale_ref[0, 0, pl.ds(crow, 1), :]) + shift_ref[0, 0, pl.ds(crow, 1), :]
    h_ref[...] = _pack_bf16_pairs(h)
    lg_ref[...] = jnp.dot(h.astype(BF16), wr_ref[...], preferred_element_type=F32) + br_ref[...]


def _outproj(x, ya, yb, yc, w_out_b, mod, layer, g2, w_router_b, b_router_p, n_ctx):
    n_tok = x.shape[0]
    tm = OUTPROJ_TM

    def mod_spec(chunk):
        return pl.BlockSpec((1, 1, COND_ROWS, D_MODEL), lambda i: (layer, chunk, 0, 0))

    return pl.pallas_call(
        functools.partial(_outproj_kernel, n_ctx=n_ctx),
        out_shape=(jax.ShapeDtypeStruct((n_tok, D_MODEL), F32),
                   jax.ShapeDtypeStruct((n_tok, PACKED_WIDTH), U32),
                   jax.ShapeDtypeStruct((n_tok, LANES), F32)),
        grid=(n_tok // tm,),
        in_specs=[
            pl.BlockSpec((tm, D_MODEL), lambda i: (i, 0)),
            pl.BlockSpec((tm, POOL_WIDTH), lambda i: (i, 0)),
            pl.BlockSpec((tm, NA_WIDTH), lambda i: (i, 0)),
            pl.BlockSpec((tm, GQA_WIDTH), lambda i: (i, 0)),
            pl.BlockSpec((1, D_MODEL, D_MODEL), lambda i: (layer, 0, 0)),
            mod_spec(2), mod_spec(3), mod_spec(4),
            pl.BlockSpec((1, D_MODEL), lambda i: (0, 0)),
            pl.BlockSpec((D_MODEL, LANES), lambda i: (0, 0)),
            pl.BlockSpec((1, LANES), lambda i: (0, 0)),
        ],
        out_specs=(pl.BlockSpec((tm, D_MODEL), lambda i: (i, 0)),
                   pl.BlockSpec((tm, PACKED_WIDTH), lambda i: (i, 0)),
                   pl.BlockSpec((tm, LANES), lambda i: (i, 0))),
        compiler_params=_cparams("arbitrary"),
        name="outproj_norm_router",
    )(x, ya, yb, yc, w_out_b, mod, mod, mod, g2, w_router_b, b_router_p)


def _route_kernel(lg_ref, code_ref, gate_ref, cnt_ref, carry_scr):
    i = pl.program_id(0)

    @pl.when(i == 0)
    def _():
        carry_scr[...] = jnp.zeros_like(carry_scr)

    tm = lg_ref.shape[0]
    l = lg_ref[...]
    lane = lax.broadcasted_iota(I32, (tm, LANES), 1)
    sel = jnp.zeros((tm, LANES), F32)
    vals, idxs = [], []
    for _ in range(TOP_K):
        m = jnp.max(l, axis=-1, keepdims=True)
        ik = jnp.min(jnp.where(l == m, lane, LANES), axis=-1, keepdims=True)
        hit = lane == ik
        sel = jnp.where(hit, 1.0, sel)
        l = jnp.where(hit, -jnp.inf, l)
        vals.append(m)
        idxs.append(ik)
    exps = [jnp.exp(v - vals[0]) for v in vals]
    tot = exps[0] + exps[1] + exps[2] + exps[3]
    r = lax.broadcasted_iota(I32, (tm, tm), 0)
    c = lax.broadcasted_iota(I32, (tm, tm), 1)
    tri = jnp.where(c < r, 1.0, 0.0).astype(BF16)
    before = jnp.dot(tri, sel.astype(BF16), preferred_element_type=F32) + carry_scr[...]
    code_out = jnp.zeros((tm, LANES), I32)
    gate_out = jnp.zeros((tm, LANES), F32)
    for k in range(TOP_K):
        pk = jnp.sum(jnp.where(lane == idxs[k], before, 0.0), axis=-1, keepdims=True)
        code_out = jnp.where(lane == k, idxs[k] * SLOT_BASE + pk.astype(I32), code_out)
        gate_out = jnp.where(lane == k, exps[k] / tot, gate_out)
    code_ref[...] = code_out
    gate_ref[...] = gate_out
    carry_scr[...] = carry_scr[...] + jnp.sum(sel, axis=0, keepdims=True)
    cnt_ref[...] = jnp.broadcast_to(carry_scr[...], cnt_ref.shape)


def _route(logits):
    n_tok = logits.shape[0]
    assert n_tok <= SLOT_BASE
    tm = ROUTE_TM
    tile = pl.BlockSpec((tm, LANES), lambda i: (i, 0))
    return pl.pallas_call(
        _route_kernel,
        out_shape=(jax.ShapeDtypeStruct((n_tok, LANES), I32),
                   jax.ShapeDtypeStruct((n_tok, LANES), F32),
                   jax.ShapeDtypeStruct((8, LANES), F32)),
        grid=(n_tok // tm,),
        in_specs=[tile],
        out_specs=(tile, tile, pl.BlockSpec((8, LANES), lambda i: (0, 0))),
        scratch_shapes=[pltpu.VMEM((1, LANES), F32)],
        compiler_params=_cparams("arbitrary"),
        name="route_topk",
    )(logits)


ZERO_ROWS = 64


def _dispatch_kernel(cnt_ref, start_ref, nused_ref, row_ref, h_ref, xs_hbm, zero_scr, sem, *, n_blocks):
    i = pl.program_id(0)
    last = pl.num_programs(0) - 1

    def row_copy(src_row, dst_row):
        return pltpu.make_async_copy(h_ref.at[pl.ds(src_row, 1)], xs_hbm.at[pl.ds(dst_row, 1)], sem)

    def zero_row_copy(dst_row):
        return pltpu.make_async_copy(zero_scr.at[pl.ds(0, 1)], xs_hbm.at[pl.ds(dst_row, 1)], sem)

    def zero_chunk_copy(dst_row):
        return pltpu.make_async_copy(zero_scr, xs_hbm.at[pl.ds(dst_row, ZERO_ROWS)], sem)

    def issue(tt, carry):
        t0 = pl.multiple_of(tt * DMA_UNROLL, DMA_UNROLL)
        for u in range(DMA_UNROLL):
            for k in range(TOP_K):
                dst = row_ref[0, 0, (t0 + u) * TOP_K + k]
                pltpu.make_async_copy(h_ref.at[pl.ds(t0, DMA_UNROLL)].at[pl.ds(u, 1)],
                                      xs_hbm.at[pl.ds(dst, 1)], sem).start(priority=k % 2)
        return carry

    lax.fori_loop(0, TOKEN_TILE // DMA_UNROLL, issue, 0)

    def drain(t, carry):
        for k in range(TOP_K):
            row_copy(0, 0).wait()
        return carry

    lax.fori_loop(0, TOKEN_TILE, drain, 0, unroll=DMA_UNROLL)

    @pl.when(i == last)
    def _():
        zero_scr[...] = jnp.zeros_like(zero_scr)

        def per_expert(e, carry):
            cnt = cnt_ref[e]
            padded = (cnt + MOE_BLOCK - 1) // MOE_BLOCK * MOE_BLOCK
            base = start_ref[e]

            def z_issue(p, c):
                zero_row_copy(base + p).start()
                return c

            def z_drain(p, c):
                zero_row_copy(0).wait()
                return c

            lax.fori_loop(cnt, padded, z_issue, 0)
            lax.fori_loop(cnt, padded, z_drain, 0)
            return carry

        lax.fori_loop(0, N_EXPERTS, per_expert, 0)

        chunks = MOE_BLOCK // ZERO_ROWS

        def t_issue(c, carry):
            zero_chunk_copy(c * ZERO_ROWS).start()
            return carry

        def t_drain(c, carry):
            zero_chunk_copy(0).wait()
            return carry

        lax.fori_loop(nused_ref[0] * chunks, n_blocks * chunks, t_issue, 0)
        lax.fori_loop(nused_ref[0] * chunks, n_blocks * chunks, t_drain, 0)


def _dispatch(counts, starts, nused, rows, h, n_blocks):
    n_tok = h.shape[0]
    nt = n_tok // TOKEN_TILE
    grid_spec = pltpu.PrefetchScalarGridSpec(
        num_scalar_prefetch=3,
        grid=(nt,),
        in_specs=[
            pl.BlockSpec((1, 1, TOKEN_TILE * TOP_K), lambda i, *_: (i, 0, 0), memory_space=pltpu.SMEM),
            pl.BlockSpec((TOKEN_TILE, PACKED_WIDTH), lambda i, *_: (i, 0)),
        ],
        out_specs=pl.BlockSpec(memory_space=pl.ANY),
        scratch_shapes=[pltpu.VMEM((ZERO_ROWS, PACKED_WIDTH), U32), pltpu.SemaphoreType.DMA],
    )
    return pl.pallas_call(
        functools.partial(_dispatch_kernel, n_blocks=n_blocks),
        out_shape=jax.ShapeDtypeStruct((n_blocks * MOE_BLOCK, PACKED_WIDTH), U32),
        grid_spec=grid_spec,
        compiler_params=_cparams("arbitrary"),
        name="moe_dispatch",
    )(counts, starts, nused, rows, h)


def _expert_kernel(blk_e_ref, nused_ref, xs_ref, wgu_ref, bgu_ref, wd_ref, bd_ref, ys_ref, wgu_scr, wd_scr):
    b = pl.program_id(0)
    e = blk_e_ref[b]
    prev = blk_e_ref[jnp.maximum(b - 1, 0)]

    @pl.when((b == 0) | (e != prev))
    def _():
        wgu_scr[...] = wgu_ref[0, 0].astype(BF16)
        wd_scr[...] = wd_ref[0, 0].astype(BF16)

    @pl.when(b < nused_ref[0])
    def _():
        x_first, x_second = _unpack_bf16_pairs(xs_ref[...])
        gu = (jnp.dot(x_first, wgu_scr[0:PACKED_WIDTH, :], preferred_element_type=F32)
              + jnp.dot(x_second, wgu_scr[PACKED_WIDTH:, :], preferred_element_type=F32) + bgu_ref[0, 0])
        gate = jnp.minimum(gu[:, :D_FF], SWIGLU_LIMIT)
        up = jnp.clip(gu[:, D_FF:], -SWIGLU_LIMIT, SWIGLU_LIMIT)
        act = (up + 1) * (gate * jax.nn.sigmoid(SWIGLU_ALPHA * gate))
        ys_ref[...] = jnp.dot(act.astype(BF16), wd_scr[...], preferred_element_type=F32) + bd_ref[0, 0]

    @pl.when(b >= nused_ref[0])
    def _():
        ys_ref[...] = jnp.zeros_like(ys_ref)


def _experts(blk_e, nused, xs, w_gate_up, b_gate_up, w_down, b_down, layer):
    n_blocks = blk_e.shape[0]
    depth = w_gate_up.shape[0]
    grid_spec = pltpu.PrefetchScalarGridSpec(
        num_scalar_prefetch=2,
        grid=(n_blocks,),
        in_specs=[
            pl.BlockSpec((MOE_BLOCK, PACKED_WIDTH), lambda b, be, nu: (jnp.minimum(b, nu[0] - 1), 0)),
            pl.BlockSpec((1, 1, D_MODEL, 2 * D_FF), lambda b, be, nu: (layer, be[b], 0, 0)),
            pl.BlockSpec((1, 1, 1, 2 * D_FF), lambda b, be, nu: (layer, be[b], 0, 0)),
            pl.BlockSpec((1, 1, D_FF, D_MODEL), lambda b, be, nu: (layer, be[b], 0, 0)),
            pl.BlockSpec((1, 1, 1, D_MODEL), lambda b, be, nu: (layer, be[b], 0, 0)),
        ],
        out_specs=pl.BlockSpec((MOE_BLOCK, D_MODEL), lambda b, be, nu: (b, 0)),
        scratch_shapes=[pltpu.VMEM((D_MODEL, 2 * D_FF), BF16), pltpu.VMEM((D_FF, D_MODEL), BF16)],
    )
    return pl.pallas_call(
        _expert_kernel,
        out_shape=jax.ShapeDtypeStruct((xs.shape[0], D_MODEL), F32),
        grid_spec=grid_spec,
        compiler_params=_cparams("arbitrary"),
        name="moe_experts",
    )(blk_e, nused, xs, w_gate_up, b_gate_up.reshape(depth, N_EXPERTS, 1, 2 * D_FF),
      w_down, b_down.reshape(depth, N_EXPERTS, 1, D_MODEL))


def _combine_kernel(row_ref, next_row_ref, gate_ref, x_ref, mgate_ref, gfin_ref, ys_hbm, o_ref,
                    rows_scr, sems, *, n_ctx, final_norm):
    i = pl.program_id(0)
    n_steps = pl.num_programs(0)
    crow = _cond_row(i, TOKEN_TILE, n_ctx)
    slot = i % 2

    def row_copy(buf, src_row, k, t):
        return pltpu.make_async_copy(ys_hbm.at[pl.ds(src_row, 1)], rows_scr.at[buf, k, pl.ds(t, 1)],
                                     sems.at[buf])

    def gather(rows, buf):
        def issue(tt, carry):
            t0 = pl.multiple_of(tt * DMA_UNROLL, DMA_UNROLL)
            for u in range(DMA_UNROLL):
                for k in range(TOP_K):
                    src = rows[0, 0, (t0 + u) * TOP_K + k]
                    pltpu.make_async_copy(ys_hbm.at[pl.ds(src, 1)],
                                          rows_scr.at[buf, k, pl.ds(t0, DMA_UNROLL)].at[pl.ds(u, 1)],
                                          sems.at[buf]).start(priority=k % 2)
            return carry

        lax.fori_loop(0, TOKEN_TILE // DMA_UNROLL, issue, 0)

    @pl.when(i == 0)
    def _():
        gather(row_ref, 0)

    @pl.when(i + 1 < n_steps)
    def _():
        gather(next_row_ref, 1 - slot)

    def drain(t, carry):
        for k in range(TOP_K):
            row_copy(slot, 0, 0, 0).wait()
        return carry

    lax.fori_loop(0, TOKEN_TILE, drain, 0, unroll=DMA_UNROLL)

    g = gate_ref[...]
    y = g[:, 0:1] * rows_scr[slot, 0]
    for k in range(1, TOP_K):
        y = y + g[:, k:k + 1] * rows_scr[slot, k]
    x = x_ref[...] + mgate_ref[0, 0, pl.ds(crow, 1), :] * y
    if final_norm:
        x = x * lax.rsqrt(jnp.mean(x * x, axis=-1, keepdims=True) + EPS) * gfin_ref[...]
    o_ref[...] = x


def _combine(rows, gates, x, mod, layer, ys, g_final, n_ctx, final_norm):
    n_tok = x.shape[0]
    nt = n_tok // TOKEN_TILE
    row_block = (1, 1, TOKEN_TILE * TOP_K)
    return pl.pallas_call(
        functools.partial(_combine_kernel, n_ctx=n_ctx, final_norm=final_norm),
        out_shape=jax.ShapeDtypeStruct((n_tok, D_MODEL), F32),
        grid=(nt,),
        in_specs=[
            pl.BlockSpec(row_block, lambda i: (i, 0, 0), memory_space=pltpu.SMEM),
            pl.BlockSpec(row_block, lambda i: (jnp.minimum(i + 1, nt - 1), 0, 0), memory_space=pltpu.SMEM),
            pl.BlockSpec((TOKEN_TILE, LANES), lambda i: (i, 0)),
            pl.BlockSpec((TOKEN_TILE, D_MODEL), lambda i: (i, 0)),
            pl.BlockSpec((1, 1, COND_ROWS, D_MODEL), lambda i: (layer, 5, 0, 0)),
            pl.BlockSpec((1, D_MODEL), lambda i: (0, 0)),
            pl.BlockSpec(memory_space=pl.ANY),
        ],
        out_specs=pl.BlockSpec((TOKEN_TILE, D_MODEL), lambda i: (i, 0)),
        scratch_shapes=[pltpu.VMEM((2, TOP_K, TOKEN_TILE, D_MODEL), F32), pltpu.SemaphoreType.DMA((2,))],
        compiler_params=_cparams("arbitrary"),
        name="moe_combine",
    )(rows, rows, gates, x, mod, g_final, ys)


def _rope_tables():
    t = np.arange(LAT_SEQ)
    row = (t // GRID_W).astype(np.float32)
    col = (t % GRID_W).astype(np.float32)
    half = HEAD_DIM // 2
    inv = jnp.asarray(ROPE_THETA, F32) ** (-jnp.arange(0, half, 2, dtype=F32) / half)
    ang = jnp.concatenate([row[:, None] * inv, col[:, None] * inv], axis=-1)
    cos, sin = jnp.cos(ang), jnp.sin(ang)
    return jnp.concatenate([cos, cos], axis=-1), jnp.concatenate([-sin, sin], axis=-1)


def _slot_rows(codes, starts):
    expert = lax.shift_right_logical(codes, SLOT_SHIFT)
    base = jnp.sum(jnp.where(expert[..., None] == jnp.arange(N_EXPERTS, dtype=I32), starts, 0), axis=-1)
    return base + (codes & (SLOT_BASE - 1))


def _routing_plan(counts_f, n_blocks):
    counts = counts_f[0, :N_EXPERTS].astype(I32)
    padded = (counts + MOE_BLOCK - 1) // MOE_BLOCK * MOE_BLOCK
    pend = jnp.cumsum(padded)
    starts = pend - padded
    nused = pend[-1] // MOE_BLOCK
    blk = jnp.minimum(jnp.arange(n_blocks, dtype=I32), nused - 1)
    blk_e = jnp.sum((pend[None, :] <= (blk * MOE_BLOCK)[:, None]).astype(I32), axis=1)
    blk_e = jnp.minimum(blk_e, N_EXPERTS - 1)
    return counts, starts.astype(I32), nused.reshape(1).astype(I32), blk_e


def _forward(x_prompt, x_sample, cache_na_k, cache_na_v, cache_gqa_k, cache_gqa_v, c, c_ctx,
             w_ada, b_ada, g_norm1, g_norm2, w_in, w_pool, pool_scale, na_rel_bias, g_q, g_k,
             w_out, w_router, b_router, w_gate_up, b_gate_up, w_down, b_down, g_final):
    depth = w_in.shape[0]
    nb_ctx, nb_lat = x_prompt.shape[0], x_sample.shape[0]
    n_ctx, n_lat = nb_ctx * CTX_SEQ, nb_lat * LAT_SEQ
    n_tok = n_ctx + n_lat
    assert x_prompt.shape[1:] == (CTX_SEQ, D_MODEL) and x_sample.shape[1:] == (LAT_SEQ, D_MODEL)
    assert n_ctx % LAT_SEQ == 0 and 1 + nb_lat <= COND_ROWS
    n_blocks = n_tok * TOP_K // MOE_BLOCK + N_EXPERTS

    x = jnp.concatenate([x_prompt.reshape(n_ctx, D_MODEL), x_sample.reshape(n_lat, D_MODEL)], axis=0)
    cond = jnp.zeros((COND_ROWS, D_MODEL), F32).at[0].set(c_ctx).at[1:1 + nb_lat].set(c)
    mod = _ada_mod(cond, w_ada, b_ada)

    cos2, sin2 = _rope_tables()
    na_bias = _na_bias_tables(na_rel_bias)
    w_in_b = w_in.astype(BF16)
    w_out_b = w_out.astype(BF16)
    w_pool_b = w_pool.astype(BF16)
    w_router_b = jnp.pad(w_router, ((0, 0), (0, 0), (0, LANES - N_EXPERTS))).astype(BF16)
    b_router_p = jnp.pad(b_router, ((0, 0), (0, LANES - N_EXPERTS)), constant_values=NEG)
    cna_k = cache_na_k.reshape(nb_lat, depth, CTX_SEQ, NA_WIDTH).astype(BF16)
    cna_v = cache_na_v.reshape(nb_lat, depth, CTX_SEQ, NA_WIDTH).astype(BF16)
    cgq_k = cache_gqa_k.reshape(nb_lat, depth, CTX_SEQ, GQA_KV_WIDTH).astype(BF16)
    cgq_v = cache_gqa_v.reshape(nb_lat, depth, CTX_SEQ, GQA_KV_WIDTH).astype(BF16)

    new_kv = []
    for l in range(depth):
        u, kv_f, proj_b = _inproj(x, mod, l, g_norm1[l][None], w_in_b, g_q[l][None], g_k[l][None],
                                  cos2, sin2, n_ctx)
        new_kv.append(kv_f[:n_ctx])
        ya = _pool(u, w_pool_b[l], pool_scale[l][None], n_ctx)
        yb = _na_attention(proj_b, cna_k, cna_v, na_bias, l, n_ctx)
        yc = _gqa_attention(proj_b, cgq_k, cgq_v, l, n_ctx)
        x, h, logits = _outproj(x, ya, yb, yc, w_out_b, mod, l, g_norm2[l][None],
                                w_router_b[l], b_router_p[l][None], n_ctx)
        codes, gates, counts_f = _route(logits)
        counts, starts, nused, blk_e = _routing_plan(counts_f, n_blocks)
        rows = _slot_rows(codes[:, :TOP_K], starts).reshape(n_tok // TOKEN_TILE, 1, TOKEN_TILE * TOP_K)
        xs = _dispatch(counts, starts, nused, rows, h, n_blocks)
        ys = _experts(blk_e, nused, xs, w_gate_up, b_gate_up, w_down, b_down, l)
        x = _combine(rows, gates, x, mod, l, ys, g_final[None], n_ctx, final_norm=(l == depth - 1))

    y = x
    y_prompt = y[:n_ctx].reshape(nb_ctx, CTX_SEQ, D_MODEL)
    y_sample = y[n_ctx:].reshape(nb_lat, LAT_SEQ, D_MODEL)

    def stack(col, width):
        per_layer = [p[:, col:col + width].reshape(nb_ctx, CTX_SEQ, width // HEAD_DIM, HEAD_DIM) for p in new_kv]
        return jnp.stack(per_layer, axis=1)

    return (y_prompt, y_sample, stack(KVF_KN, NA_WIDTH), stack(KVF_VN, NA_WIDTH),
            stack(KVF_KG, GQA_KV_WIDTH), stack(KVF_VG, GQA_KV_WIDTH))


def kernel(x_prompt, x_sample, cache_na_k, cache_na_v, cache_gqa_k, cache_gqa_v, c, c_ctx, w_ada, b_ada,
           g_norm1, g_norm2, w_in, w_pool, pool_scale, na_rel_bias, g_q, g_k, w_out, w_router, b_router,
           w_gate_up, b_gate_up, w_down, b_down, g_final):
    return _forward(x_prompt, x_sample, cache_na_k, cache_na_v, cache_gqa_k, cache_gqa_v, c, c_ctx,
                    w_ada, b_ada, g_norm1, g_norm2, w_in, w_pool, pool_scale, na_rel_bias, g_q, g_k,
                    w_out, w_router, b_router, w_gate_up, b_gate_up, w_down, b_down, g_final)
```

```python
import functools
import math

import numpy as np
import jax
import jax.numpy as jnp
from jax import lax
from jax.experimental import pallas as pl
from jax.experimental.pallas import tpu as pltpu

F32 = jnp.float32
BF16 = jnp.bfloat16
I32 = jnp.int32

D_MODEL = 2048
HEAD_DIM = 128
CTX_SEQ = 256
LAT_SEQ = 2048
GRID_W = 64
GRID_ROWS = LAT_SEQ // GRID_W
POOL_WINDOWS = (2, 4, 8, 16)
POOL_GROUP = 128
POOL_WIDTH = 512
NA_HEADS = 4
NA_WIDTH = 512
NA_ROWS = 8
NA_COLS = 16
GQA_HEADS = 8
GQA_KV_HEADS = 2
GQA_GROUP = GQA_HEADS // GQA_KV_HEADS
GQA_WIDTH = 1024
GQA_KV_WIDTH = 256
IN_WIDTH = 3584
N_EXPERTS = 32
TOP_K = 4
D_FF = 512
SWIGLU_LIMIT = 7.0
SWIGLU_ALPHA = 1.702
ROPE_THETA = 10000.0
EPS = 1e-6
NEG = -1e30

COL_U, COL_QN, COL_KN, COL_VN, COL_QG, COL_KG, COL_VG = 0, 512, 1024, 1536, 2048, 3072, 3328
ACT_QG, ACT_QN, ACT_KN, ACT_VN, ACT_KG, ACT_VG, ACT_WIDTH = 0, 1024, 1536, 2048, 2560, 2816, 3072
KVF_KN, KVF_VN, KVF_KG, KVF_VG, KVF_WIDTH = 0, 512, 1024, 1280, 1536

LANES = 128
COND_ROWS = 16
ATT_TILE = 256
NA_TILE_ROWS = ATT_TILE // GRID_W
NA_WIN_ROWS = NA_ROWS + NA_TILE_ROWS - 1
NA_WIN = NA_WIN_ROWS * GRID_W
INPROJ_TM = 512
OUTPROJ_TM = 512
ROUTE_TM = 512
MOE_BLOCK = 512
TOKEN_TILE = 256
DISPATCH_TILE = 512
SLOT_SHIFT = 16
SLOT_BASE = 1 << SLOT_SHIFT
DMA_UNROLL = 8
VMEM_LIMIT = 56 * 1024 * 1024


def _cparams(*sem):
    return pltpu.CompilerParams(dimension_semantics=sem, vmem_limit_bytes=VMEM_LIMIT)


def _cond_row(tile, tile_rows, n_ctx):
    row0 = tile * tile_rows
    return jnp.where(row0 < n_ctx, 0, 1 + (row0 - n_ctx) // LAT_SEQ)


def _head(h):
    return slice(h * HEAD_DIM, (h + 1) * HEAD_DIM)


U32 = jnp.uint32
PACKED_WIDTH = D_MODEL // 2


def _pack_bf16_pairs(x):
    half = x.shape[1] // 2
    hi = pltpu.bitcast(x[:, :half].astype(BF16).astype(F32), U32)
    lo = pltpu.bitcast(x[:, half:].astype(BF16).astype(F32), U32)
    return hi | (lo >> 16)


def _unpack_bf16_pairs(p):
    first = pltpu.bitcast(p & jnp.uint32(0xFFFF0000), F32).astype(BF16)
    second = pltpu.bitcast(p << 16, F32).astype(BF16)
    return first, second


def _ada_kernel(c_ref, w_ref, b_ref, o_ref):
    c = c_ref[...]
    s = (c * jax.nn.sigmoid(c)).astype(BF16)
    o_ref[0, 0] = jnp.dot(s, w_ref[0].astype(BF16), preferred_element_type=F32) + b_ref[0]


def _ada_mod(cond, w_ada, b_ada):
    depth = w_ada.shape[0]
    tn = 1024
    per = D_MODEL // tn
    return pl.pallas_call(
        _ada_kernel,
        out_shape=jax.ShapeDtypeStruct((depth, 6, COND_ROWS, D_MODEL), F32),
        grid=(depth, 6 * per),
        in_specs=[
            pl.BlockSpec((COND_ROWS, D_MODEL), lambda l, j: (0, 0)),
            pl.BlockSpec((1, D_MODEL, tn), lambda l, j: (l, 0, j)),
            pl.BlockSpec((1, 1, tn), lambda l, j: (l, 0, j)),
        ],
        out_specs=pl.BlockSpec((1, 1, COND_ROWS, tn), lambda l, j: (l, j // per, 0, j % per)),
        compiler_params=_cparams("arbitrary", "arbitrary"),
        name="ada_mod",
    )(cond, w_ada, b_ada.reshape(depth, 1, 6 * D_MODEL))


def _head_rms(x, g):
    return x * lax.rsqrt(jnp.mean(x * x, axis=-1, keepdims=True) + EPS) * g


NORM_CHUNKS = 8


def _inproj_kernel(x0_ref, xn_ref, shift_ref, scale_ref, g1_ref, w_ref, gq_ref, gk_ref, cos_ref, sin_ref,
                   u_ref, kv_ref, act_ref, h_even, h_odd, *, n_ctx):
    i = pl.program_id(0)
    nxt = jnp.minimum(i + 1, pl.num_programs(0) - 1)
    is_lat = i * INPROJ_TM >= n_ctx
    cos = jnp.where(is_lat, cos_ref[...], 1.0)
    sin = jnp.where(is_lat, sin_ref[...], 0.0)

    def normalize(x_ref, tile, h_ref, rows):
        crow = _cond_row(tile, INPROJ_TM, n_ctx)
        x = x_ref[rows, :]
        y = x * lax.rsqrt(jnp.mean(x * x, axis=-1, keepdims=True) + EPS) * g1_ref[...]
        h_ref[rows, :] = (y * (1 + scale_ref[0, 0, pl.ds(crow, 1), :])
                          + shift_ref[0, 0, pl.ds(crow, 1), :]).astype(BF16)

    def normed_rope(xh, g_ref):
        xh = _head_rms(xh, g_ref[...])
        return xh * cos + pltpu.roll(xh, HEAD_DIM // 2, 1) * sin

    def project(h_ref, h_next_ref):
        chunk = INPROJ_TM // NORM_CHUNKS
        steps = iter(range(NORM_CHUNKS))

        def proj(col, width):
            acc = jnp.dot(h_ref[...], w_ref[0, :, col:col + width], preferred_element_type=F32)
            c = next(steps)
            normalize(xn_ref, nxt, h_next_ref, slice(c * chunk, (c + 1) * chunk))
            return acc

        for half in range(GQA_KV_HEADS):
            q = proj(COL_QG + half * GQA_GROUP * HEAD_DIM, GQA_GROUP * HEAD_DIM)
            for g in range(GQA_GROUP):
                col = ACT_QG + (half * GQA_GROUP + g) * HEAD_DIM
                act_ref[:, col:col + HEAD_DIM] = normed_rope(q[:, _head(g)], gq_ref).astype(BF16)
        kg = proj(COL_KG, GQA_KV_WIDTH)
        for kh in range(GQA_KV_HEADS):
            v = normed_rope(kg[:, _head(kh)], gk_ref)
            kv_ref[:, KVF_KG + kh * HEAD_DIM:KVF_KG + (kh + 1) * HEAD_DIM] = v
            act_ref[:, ACT_KG + kh * HEAD_DIM:ACT_KG + (kh + 1) * HEAD_DIM] = v.astype(BF16)
        vg = proj(COL_VG, GQA_KV_WIDTH)
        kv_ref[:, KVF_VG:KVF_VG + GQA_KV_WIDTH] = vg
        act_ref[:, ACT_VG:ACT_VG + GQA_KV_WIDTH] = vg.astype(BF16)
        for col, kv_col, act_col in ((COL_KN, KVF_KN, ACT_KN), (COL_VN, KVF_VN, ACT_VN)):
            v = proj(col, NA_WIDTH)
            kv_ref[:, kv_col:kv_col + NA_WIDTH] = v
            act_ref[:, act_col:act_col + NA_WIDTH] = v.astype(BF16)
        act_ref[:, ACT_QN:ACT_QN + NA_WIDTH] = proj(COL_QN, NA_WIDTH).astype(BF16)
        u_ref[...] = proj(COL_U, POOL_WIDTH)

    @pl.when(i == 0)
    def _():
        normalize(x0_ref, 0, h_even, slice(None))

    @pl.when(i % 2 == 0)
    def _():
        project(h_even, h_odd)

    @pl.when(i % 2 == 1)
    def _():
        project(h_odd, h_even)


def _inproj(x, mod, layer, g1, w_in_b, g_q, g_k, cos2, sin2, n_ctx):
    n_tok = x.shape[0]
    tm = INPROJ_TM
    lat_tiles = LAT_SEQ // tm
    ctx_tiles = n_ctx // tm

    def rope_map(i):
        return (jnp.maximum(i - ctx_tiles, 0) % lat_tiles, 0)

    def rows(width):
        return pl.BlockSpec((tm, width), lambda i: (i, 0))

    return pl.pallas_call(
        functools.partial(_inproj_kernel, n_ctx=n_ctx),
        out_shape=(jax.ShapeDtypeStruct((n_tok, POOL_WIDTH), F32),
                   jax.ShapeDtypeStruct((n_tok, KVF_WIDTH), F32),
                   jax.ShapeDtypeStruct((n_tok, ACT_WIDTH), BF16)),
        grid=(n_tok // tm,),
        in_specs=[
            pl.BlockSpec((tm, D_MODEL), lambda i: (0, 0)),
            pl.BlockSpec((tm, D_MODEL), lambda i: (jnp.minimum(i + 1, n_tok // tm - 1), 0)),
            pl.BlockSpec((1, 1, COND_ROWS, D_MODEL), lambda i: (layer, 0, 0, 0)),
            pl.BlockSpec((1, 1, COND_ROWS, D_MODEL), lambda i: (layer, 1, 0, 0)),
            pl.BlockSpec((1, D_MODEL), lambda i: (0, 0)),
            pl.BlockSpec((1, D_MODEL, IN_WIDTH), lambda i: (layer, 0, 0), pipeline_mode=pl.Buffered(1)),
            pl.BlockSpec((1, HEAD_DIM), lambda i: (0, 0)),
            pl.BlockSpec((1, HEAD_DIM), lambda i: (0, 0)),
            pl.BlockSpec((tm, HEAD_DIM), rope_map),
            pl.BlockSpec((tm, HEAD_DIM), rope_map),
        ],
        out_specs=(rows(POOL_WIDTH), rows(KVF_WIDTH), rows(ACT_WIDTH)),
        scratch_shapes=[pltpu.VMEM((tm, D_MODEL), BF16), pltpu.VMEM((tm, D_MODEL), BF16)],
        compiler_params=_cparams("arbitrary"),
        name="norm_inproj",
    )(x, x, mod, mod, g1, w_in_b, g_q, g_k, cos2, sin2)


POOL_HALO = 8


def _pool_kernel(up_ref, uc_ref, un_ref, wp_ref, ps_ref, o_ref, pad_scr, *, n_ctx):
    i = pl.program_id(0)
    row0 = i * ATT_TILE
    is_ctx = row0 < n_ctx
    t0 = jnp.where(is_ctx, 0, (row0 - n_ctx) % LAT_SEQ)
    seq = jnp.where(is_ctx, CTX_SEQ, LAT_SEQ)
    first = t0 == 0
    last = t0 + ATT_TILE == seq
    zeros = jnp.zeros((POOL_HALO, POOL_WIDTH), F32)
    pad_scr[0:POOL_HALO, :] = jnp.where(first, zeros, up_ref[ATT_TILE - POOL_HALO:ATT_TILE, :])
    pad_scr[POOL_HALO:POOL_HALO + ATT_TILE, :] = uc_ref[...]
    pad_scr[POOL_HALO + ATT_TILE:, :] = jnp.where(last, zeros, un_ref[0:POOL_HALO, :])
    t = t0 + lax.broadcasted_iota(I32, (ATT_TILE, 1), 0)
    for g, w in enumerate(POOL_WINDOWS):
        cols = slice(g * POOL_GROUP, (g + 1) * POOL_GROUP)
        acc = pad_scr[pl.ds(POOL_HALO - w // 2, ATT_TILE), cols]
        for d in range(-w // 2 + 1, w // 2):
            acc = acc + pad_scr[pl.ds(POOL_HALO + d, ATT_TILE), cols]
        cnt = (jnp.minimum(t + w // 2, seq) - jnp.maximum(t - w // 2, 0)).astype(F32)
        pooled = acc / cnt - uc_ref[:, cols]
        y = jnp.dot(pooled.astype(BF16), wp_ref[g], preferred_element_type=F32)
        o_ref[:, cols] = (y * ps_ref[:, cols]).astype(BF16)


def _pool(proj_f, w_pool_b, pool_scale, n_ctx):
    n_tok = proj_f.shape[0]
    nt = n_tok // ATT_TILE
    return pl.pallas_call(
        functools.partial(_pool_kernel, n_ctx=n_ctx),
        out_shape=jax.ShapeDtypeStruct((n_tok, POOL_WIDTH), BF16),
        grid=(nt,),
        in_specs=[
            pl.BlockSpec((ATT_TILE, POOL_WIDTH), lambda i: (jnp.maximum(i - 1, 0), 0)),
            pl.BlockSpec((ATT_TILE, POOL_WIDTH), lambda i: (i, 0)),
            pl.BlockSpec((ATT_TILE, POOL_WIDTH), lambda i: (jnp.minimum(i + 1, nt - 1), 0)),
            pl.BlockSpec((len(POOL_WINDOWS), POOL_GROUP, POOL_GROUP), lambda i: (0, 0, 0)),
            pl.BlockSpec((1, POOL_WIDTH), lambda i: (0, 0)),
        ],
        out_specs=pl.BlockSpec((ATT_TILE, POOL_WIDTH), lambda i: (i, 0)),
        scratch_shapes=[pltpu.VMEM((ATT_TILE + 2 * POOL_HALO, POOL_WIDTH), F32)],
        compiler_params=_cparams("arbitrary"),
        name="pool_mixer",
    )(proj_f, proj_f, proj_f, w_pool_b, pool_scale)


_NT_DIMS = (((1,), (1,)), ((), ()))
ATT_SCALE = HEAD_DIM ** -0.5
EXP2_SCALE = ATT_SCALE * math.log2(math.e)


def _scores(q, k):
    return lax.dot_general(q, k, _NT_DIMS, preferred_element_type=F32)


def _attend_one(q, k, v):
    s = _scores(q, k)
    p = jnp.exp2((s - jnp.max(s, axis=-1, keepdims=True)) * EXP2_SCALE)
    o = jnp.dot(p.astype(BF16), v, preferred_element_type=F32)
    return o / jnp.sum(p, axis=-1, keepdims=True)


def _attend_two(s1, v1, s2, v2):
    m = jnp.maximum(jnp.max(s1, axis=-1, keepdims=True), jnp.max(s2, axis=-1, keepdims=True))
    p1 = jnp.exp2((s1 - m) * EXP2_SCALE)
    p2 = jnp.exp2((s2 - m) * EXP2_SCALE)
    o = (jnp.dot(p1.astype(BF16), v1, preferred_element_type=F32)
         + jnp.dot(p2.astype(BF16), v2, preferred_element_type=F32))
    return o / (jnp.sum(p1, axis=-1, keepdims=True) + jnp.sum(p2, axis=-1, keepdims=True))


def _na_kernel(q_ref, kc_ref, vc_ref, kl_ref, vl_ref, ck_ref, cv_ref, bias_ref, o_ref, *, n_ctx_steps):
    i = pl.program_id(0)

    @pl.when(i < n_ctx_steps)
    def _():
        for h in range(NA_HEADS):
            o = _attend_one(q_ref[:, _head(h)], kc_ref[:, _head(h)], vc_ref[:, _head(h)])
            o_ref[:, _head(h)] = o.astype(BF16)

    @pl.when(i >= n_ctx_steps)
    def _():
        tile = (i - n_ctx_steps) % (LAT_SEQ // ATT_TILE)
        win_row = jnp.clip(tile * NA_TILE_ROWS - NA_ROWS // 2, 0, GRID_ROWS - NA_WIN_ROWS)
        key0 = pl.multiple_of(win_row * GRID_W, GRID_W)
        def scores(h):
            q = q_ref[:, _head(h)]
            return (_scores(q, kl_ref[pl.ds(key0, NA_WIN), _head(h)]) + bias_ref[0, 0, h],
                    _scores(q, ck_ref[0, 0, :, _head(h)]))

        nxt = scores(0)
        for h in range(NA_HEADS):
            s_lat, s_ctx = nxt
            if h + 1 < NA_HEADS:
                nxt = scores(h + 1)
            o = _attend_two(s_lat, vl_ref[pl.ds(key0, NA_WIN), _head(h)],
                            s_ctx, cv_ref[0, 0, :, _head(h)])
            o_ref[:, _head(h)] = o.astype(BF16)


def _na_bias_tables(rel_bias):
    depth = rel_bias.shape[0]
    half = NA_ROWS // 2
    r0 = np.array([0, half, GRID_ROWS - NA_TILE_ROWS])[:, None, None]
    win = np.clip(r0 - half, 0, GRID_ROWS - NA_WIN_ROWS)
    qr = r0 + np.arange(NA_TILE_ROWS)[None, :, None]
    kr = win + np.arange(NA_WIN_ROWS)[None, None, :]
    rs = np.clip(qr - half, 0, GRID_ROWS - NA_ROWS)
    row_ok = (kr >= rs) & (kr < rs + NA_ROWS)
    dr = np.clip(kr - qr + NA_ROWS - 1, 0, 2 * NA_ROWS - 2)
    qc = np.arange(GRID_W)[:, None]
    kc = np.arange(GRID_W)[None, :]
    cs = np.clip(qc - NA_COLS // 2, 0, GRID_W - NA_COLS)
    col_ok = (kc >= cs) & (kc < cs + NA_COLS)
    dc = np.clip(kc - qc + NA_COLS - 1, 0, 2 * NA_COLS - 2)
    sel_r = (dr[..., None] == np.arange(2 * NA_ROWS - 1)).astype(np.float32)
    sel_c = (dc[..., None] == np.arange(2 * NA_COLS - 1)).astype(np.float32)
    vals = jnp.einsum("sqka,lhab,xyb->lshqxky", sel_r, rel_bias, sel_c, precision=lax.Precision.HIGHEST)
    ok = row_ok[:, :, None, :, None] & col_ok[None, None, :, None, :]
    table = jnp.where(ok[None, :, None], vals / ATT_SCALE, NEG)
    return table.reshape(depth, 3, NA_HEADS, ATT_TILE, NA_WIN).astype(F32)


def _na_attention(proj_b, cache_k, cache_v, bias, layer, n_ctx):
    n_tok = proj_b.shape[0]
    n_ctx_steps = n_ctx // ATT_TILE
    lat_tiles = LAT_SEQ // ATT_TILE
    lat_blk0 = n_ctx // LAT_SEQ
    wq = NA_WIDTH

    def ctx_blk(i):
        return jnp.minimum(i, n_ctx_steps - 1)

    def req(i):
        return jnp.maximum(i - n_ctx_steps, 0) // lat_tiles

    def kind(i):
        tile = jnp.maximum(i - n_ctx_steps, 0) % lat_tiles
        return jnp.where(tile == 0, 0, jnp.where(tile == lat_tiles - 1, 2, 1))

    return pl.pallas_call(
        functools.partial(_na_kernel, n_ctx_steps=n_ctx_steps),
        out_shape=jax.ShapeDtypeStruct((n_tok, NA_WIDTH), BF16),
        grid=(n_tok // ATT_TILE,),
        in_specs=[
            pl.BlockSpec((ATT_TILE, wq), lambda i: (i, ACT_QN // wq)),
            pl.BlockSpec((ATT_TILE, wq), lambda i: (ctx_blk(i), ACT_KN // wq)),
            pl.BlockSpec((ATT_TILE, wq), lambda i: (ctx_blk(i), ACT_VN // wq)),
            pl.BlockSpec((LAT_SEQ, wq), lambda i: (lat_blk0 + req(i), ACT_KN // wq)),
            pl.BlockSpec((LAT_SEQ, wq), lambda i: (lat_blk0 + req(i), ACT_VN // wq)),
            pl.BlockSpec((1, 1, CTX_SEQ, wq), lambda i: (req(i), layer, 0, 0)),
            pl.BlockSpec((1, 1, CTX_SEQ, wq), lambda i: (req(i), layer, 0, 0)),
            pl.BlockSpec((1, 1, NA_HEADS, ATT_TILE, NA_WIN), lambda i: (layer, kind(i), 0, 0, 0)),
        ],
        out_specs=pl.BlockSpec((ATT_TILE, wq), lambda i: (i, 0)),
        compiler_params=_cparams("arbitrary"),
        name="na_attention",
    )(proj_b, proj_b, proj_b, proj_b, proj_b, cache_k, cache_v, bias)


def _gqa_kernel(q_ref, kc_ref, vc_ref, kl_ref, vl_ref, ck_ref, cv_ref, o_ref, *, n_ctx_steps):
    i = pl.program_id(0)

    def kv_head(h):
        return _head(h // GQA_GROUP)

    @pl.when(i < n_ctx_steps)
    def _():
        for h in range(GQA_HEADS):
            o = _attend_one(q_ref[:, _head(h)], kc_ref[:, kv_head(h)], vc_ref[:, kv_head(h)])
            o_ref[:, _head(h)] = o.astype(BF16)

    @pl.when(i >= n_ctx_steps)
    def _():
        def scores(h):
            q = q_ref[:, _head(h)]
            return _scores(q, kl_ref[:, kv_head(h)]), _scores(q, ck_ref[0, 0, :, kv_head(h)])

        nxt = scores(0)
        for h in range(GQA_HEADS):
            s_lat, s_ctx = nxt
            if h + 1 < GQA_HEADS:
                nxt = scores(h + 1)
            o = _attend_two(s_lat, vl_ref[:, kv_head(h)], s_ctx, cv_ref[0, 0, :, kv_head(h)])
            o_ref[:, _head(h)] = o.astype(BF16)


def _gqa_attention(proj_b, cache_k, cache_v, layer, n_ctx):
    n_tok = proj_b.shape[0]
    n_ctx_steps = n_ctx // ATT_TILE
    lat_tiles = LAT_SEQ // ATT_TILE
    lat_blk0 = n_ctx // LAT_SEQ
    wkv = GQA_KV_WIDTH

    def ctx_blk(i):
        return jnp.minimum(i, n_ctx_steps - 1)

    def req(i):
        return jnp.maximum(i - n_ctx_steps, 0) // lat_tiles

    return pl.pallas_call(
        functools.partial(_gqa_kernel, n_ctx_steps=n_ctx_steps),
        out_shape=jax.ShapeDtypeStruct((n_tok, GQA_WIDTH), BF16),
        grid=(n_tok // ATT_TILE,),
        in_specs=[
            pl.BlockSpec((ATT_TILE, GQA_WIDTH), lambda i: (i, ACT_QG // GQA_WIDTH)),
            pl.BlockSpec((ATT_TILE, wkv), lambda i: (ctx_blk(i), ACT_KG // wkv)),
            pl.BlockSpec((ATT_TILE, wkv), lambda i: (ctx_blk(i), ACT_VG // wkv)),
            pl.BlockSpec((LAT_SEQ, wkv), lambda i: (lat_blk0 + req(i), ACT_KG // wkv)),
            pl.BlockSpec((LAT_SEQ, wkv), lambda i: (lat_blk0 + req(i), ACT_VG // wkv)),
            pl.BlockSpec((1, 1, CTX_SEQ, wkv), lambda i: (req(i), layer, 0, 0)),
            pl.BlockSpec((1, 1, CTX_SEQ, wkv), lambda i: (req(i), layer, 0, 0)),
        ],
        out_specs=pl.BlockSpec((ATT_TILE, GQA_WIDTH), lambda i: (i, 0)),
        compiler_params=_cparams("arbitrary"),
        name="gqa_attention",
    )(proj_b, proj_b, proj_b, proj_b, proj_b, cache_k, cache_v)


def _outproj_kernel(x_ref, ya_ref, yb_ref, yc_ref, wo_ref, gate_ref, shift_ref, scale_ref, g2_ref,
                    wr_ref, br_ref, xo_ref, h_ref, lg_ref, *, n_ctx):
    crow = _cond_row(pl.program_id(0), OUTPROJ_TM, n_ctx)
    mix = (jnp.dot(ya_ref[...], wo_ref[0, 0:POOL_WIDTH, :], preferred_element_type=F32)
           + jnp.dot(yb_ref[...], wo_ref[0, POOL_WIDTH:POOL_WIDTH + NA_WIDTH, :], preferred_element_type=F32)
           + jnp.dot(yc_ref[...], wo_ref[0, POOL_WIDTH + NA_WIDTH:, :], preferred_element_type=F32))
    x = x_ref[...] + gate_ref[0, 0, pl.ds(crow, 1), :] * mix
    xo_ref[...] = x
    y = x * lax.rsqrt(jnp.mean(x * x, axis=-1, keepdims=True) + EPS) * g2_ref[...]
    h = y * (1 + scale_ref[0, 0, pl.ds(crow, 1), :]) + shift_ref[0, 0, pl.ds(crow, 1), :]
    h_ref[...] = _pack_bf16_pairs(h)
    lg_ref[...] = jnp.dot(h.astype(BF16), wr_ref[...], preferred_element_type=F32) + br_ref[...]


def _outproj(x, ya, yb, yc, w_out_b, mod, layer, g2, w_router_b, b_router_p, n_ctx):
    n_tok = x.shape[0]
    tm = OUTPROJ_TM

    def mod_spec(chunk):
        return pl.BlockSpec((1, 1, COND_ROWS, D_MODEL), lambda i: (layer, chunk, 0, 0))

    return pl.pallas_call(
        functools.partial(_outproj_kernel, n_ctx=n_ctx),
        out_shape=(jax.ShapeDtypeStruct((n_tok, D_MODEL), F32),
                   jax.ShapeDtypeStruct((n_tok, PACKED_WIDTH), U32),
                   jax.ShapeDtypeStruct((n_tok, LANES), F32)),
        grid=(n_tok // tm,),
        in_specs=[
            pl.BlockSpec((tm, D_MODEL), lambda i: (i, 0)),
            pl.BlockSpec((tm, POOL_WIDTH), lambda i: (i, 0)),
            pl.BlockSpec((tm, NA_WIDTH), lambda i: (i, 0)),
            pl.BlockSpec((tm, GQA_WIDTH), lambda i: (i, 0)),
            pl.BlockSpec((1, D_MODEL, D_MODEL), lambda i: (layer, 0, 0)),
            mod_spec(2), mod_spec(3), mod_spec(4),
            pl.BlockSpec((1, D_MODEL), lambda i: (0, 0)),
            pl.BlockSpec((D_MODEL, LANES), lambda i: (0, 0)),
            pl.BlockSpec((1, LANES), lambda i: (0, 0)),
        ],
        out_specs=(pl.BlockSpec((tm, D_MODEL), lambda i: (i, 0)),
                   pl.BlockSpec((tm, PACKED_WIDTH), lambda i: (i, 0)),
                   pl.BlockSpec((tm, LANES), lambda i: (i, 0))),
        compiler_params=_cparams("arbitrary"),
        name="outproj_norm_router",
    )(x, ya, yb, yc, w_out_b, mod, mod, mod, g2, w_router_b, b_router_p)


def _route_kernel(lg_ref, code_ref, gate_ref, cnt_ref, carry_scr):
    i = pl.program_id(0)

    @pl.when(i == 0)
    def _():
        carry_scr[...] = jnp.zeros_like(carry_scr)

    tm = lg_ref.shape[0]
    l = lg_ref[...]
    lane = lax.broadcasted_iota(I32, (tm, LANES), 1)
    sel = jnp.zeros((tm, LANES), F32)
    vals, idxs = [], []
    for _ in range(TOP_K):
        m = jnp.max(l, axis=-1, keepdims=True)
        ik = jnp.min(jnp.where(l == m, lane, LANES), axis=-1, keepdims=True)
        hit = lane == ik
        sel = jnp.where(hit, 1.0, sel)
        l = jnp.where(hit, -jnp.inf, l)
        vals.append(m)
        idxs.append(ik)
    exps = [jnp.exp(v - vals[0]) for v in vals]
    tot = exps[0] + exps[1] + exps[2] + exps[3]
    r = lax.broadcasted_iota(I32, (tm, tm), 0)
    c = lax.broadcasted_iota(I32, (tm, tm), 1)
    tri = jnp.where(c < r, 1.0, 0.0).astype(BF16)
    before = jnp.dot(tri, sel.astype(BF16), preferred_element_type=F32) + carry_scr[...]
    code_out = jnp.zeros((tm, LANES), I32)
    gate_out = jnp.zeros((tm, LANES), F32)
    for k in range(TOP_K):
        pk = jnp.sum(jnp.where(lane == idxs[k], before, 0.0), axis=-1, keepdims=True)
        code_out = jnp.where(lane == k, idxs[k] * SLOT_BASE + pk.astype(I32), code_out)
        gate_out = jnp.where(lane == k, exps[k] / tot, gate_out)
    code_ref[...] = code_out
    gate_ref[...] = gate_out
    carry_scr[...] = carry_scr[...] + jnp.sum(sel, axis=0, keepdims=True)
    cnt_ref[...] = jnp.broadcast_to(carry_scr[...], cnt_ref.shape)


def _route(logits):
    n_tok = logits.shape[0]
    assert n_tok <= SLOT_BASE
    tm = ROUTE_TM
    tile = pl.BlockSpec((tm, LANES), lambda i: (i, 0))
    return pl.pallas_call(
        _route_kernel,
        out_shape=(jax.ShapeDtypeStruct((n_tok, LANES), I32),
                   jax.ShapeDtypeStruct((n_tok, LANES), F32),
                   jax.ShapeDtypeStruct((8, LANES), F32)),
        grid=(n_tok // tm,),
        in_specs=[tile],
        out_specs=(tile, tile, pl.BlockSpec((8, LANES), lambda i: (0, 0))),
        scratch_shapes=[pltpu.VMEM((1, LANES), F32)],
        compiler_params=_cparams("arbitrary"),
        name="route_topk",
    )(logits)


ZERO_ROWS = 64


def _dispatch_kernel(cnt_ref, start_ref, nused_ref, row_ref, h_ref, xs_hbm, zero_scr, sem, *, n_blocks):
    i = pl.program_id(0)
    last = pl.num_programs(0) - 1

    def row_copy(src_row, dst_row):
        return pltpu.make_async_copy(h_ref.at[pl.ds(src_row, 1)], xs_hbm.at[pl.ds(dst_row, 1)], sem)

    def zero_row_copy(dst_row):
        return pltpu.make_async_copy(zero_scr.at[pl.ds(0, 1)], xs_hbm.at[pl.ds(dst_row, 1)], sem)

    def zero_chunk_copy(dst_row):
        return pltpu.make_async_copy(zero_scr, xs_hbm.at[pl.ds(dst_row, ZERO_ROWS)], sem)

    def issue(tt, carry):
        t0 = pl.multiple_of(tt * DMA_UNROLL, DMA_UNROLL)
        for u in range(DMA_UNROLL):
            for k in range(TOP_K):
                dst = row_ref[0, 0, (t0 + u) * TOP_K + k]
                pltpu.make_async_copy(h_ref.at[pl.ds(t0, DMA_UNROLL)].at[pl.ds(u, 1)],
                                      xs_hbm.at[pl.ds(dst, 1)], sem).start(priority=k % 2)
        return carry

    lax.fori_loop(0, DISPATCH_TILE // DMA_UNROLL, issue, 0)

    def drain(t, carry):
        for k in range(TOP_K):
            row_copy(0, 0).wait()
        return carry

    lax.fori_loop(0, DISPATCH_TILE, drain, 0, unroll=DMA_UNROLL)

    @pl.when(i == last)
    def _():
        zero_scr[...] = jnp.zeros_like(zero_scr)

        def per_expert(e, carry):
            cnt = cnt_ref[e]
            padded = (cnt + MOE_BLOCK - 1) // MOE_BLOCK * MOE_BLOCK
            base = start_ref[e]

            def z_issue(p, c):
                zero_row_copy(base + p).start()
                return c

            def z_drain(p, c):
                zero_row_copy(0).wait()
                return c

            lax.fori_loop(cnt, padded, z_issue, 0)
            lax.fori_loop(cnt, padded, z_drain, 0)
            return carry

        lax.fori_loop(0, N_EXPERTS, per_expert, 0)

        chunks = MOE_BLOCK // ZERO_ROWS

        def t_issue(c, carry):
            zero_chunk_copy(c * ZERO_ROWS).start()
            return carry

        def t_drain(c, carry):
            zero_chunk_copy(0).wait()
            return carry

        lax.fori_loop(nused_ref[0] * chunks, n_blocks * chunks, t_issue, 0)
        lax.fori_loop(nused_ref[0] * chunks, n_blocks * chunks, t_drain, 0)


def _dispatch(counts, starts, nused, rows, h, n_blocks):
    n_tok = h.shape[0]
    nt = n_tok // DISPATCH_TILE
    grid_spec = pltpu.PrefetchScalarGridSpec(
        num_scalar_prefetch=3,
        grid=(nt,),
        in_specs=[
            pl.BlockSpec((1, 1, DISPATCH_TILE * TOP_K), lambda i, *_: (i, 0, 0), memory_space=pltpu.SMEM),
            pl.BlockSpec((DISPATCH_TILE, PACKED_WIDTH), lambda i, *_: (i, 0)),
        ],
        out_specs=pl.BlockSpec(memory_space=pl.ANY),
        scratch_shapes=[pltpu.VMEM((ZERO_ROWS, PACKED_WIDTH), U32), pltpu.SemaphoreType.DMA],
    )
    return pl.pallas_call(
        functools.partial(_dispatch_kernel, n_blocks=n_blocks),
        out_shape=jax.ShapeDtypeStruct((n_blocks * MOE_BLOCK, PACKED_WIDTH), U32),
        grid_spec=grid_spec,
        compiler_params=_cparams("arbitrary"),
        name="moe_dispatch",
    )(counts, starts, nused, rows.reshape(nt, 1, DISPATCH_TILE * TOP_K), h)


def _expert_kernel(blk_e_ref, nused_ref, xs_ref, wgu_ref, bgu_ref, wd_ref, bd_ref, ys_ref, wgu_scr, wd_scr):
    b = pl.program_id(0)
    e = blk_e_ref[b]
    prev = blk_e_ref[jnp.maximum(b - 1, 0)]

    @pl.when((b == 0) | (e != prev))
    def _():
        wgu_scr[...] = wgu_ref[0, 0].astype(BF16)
        wd_scr[...] = wd_ref[0, 0].astype(BF16)

    @pl.when(b < nused_ref[0])
    def _():
        x_first, x_second = _unpack_bf16_pairs(xs_ref[...])
        gu = (jnp.dot(x_first, wgu_scr[0:PACKED_WIDTH, :], preferred_element_type=F32)
              + jnp.dot(x_second, wgu_scr[PACKED_WIDTH:, :], preferred_element_type=F32) + bgu_ref[0, 0])
        gate = jnp.minimum(gu[:, :D_FF], SWIGLU_LIMIT)
        up = jnp.clip(gu[:, D_FF:], -SWIGLU_LIMIT, SWIGLU_LIMIT)
        act = (up + 1) * (gate * jax.nn.sigmoid(SWIGLU_ALPHA * gate))
        ys_ref[...] = jnp.dot(act.astype(BF16), wd_scr[...], preferred_element_type=F32) + bd_ref[0, 0]

    @pl.when(b >= nused_ref[0])
    def _():
        ys_ref[...] = jnp.zeros_like(ys_ref)


def _experts(blk_e, nused, xs, w_gate_up, b_gate_up, w_down, b_down, layer):
    n_blocks = blk_e.shape[0]
    depth = w_gate_up.shape[0]
    grid_spec = pltpu.PrefetchScalarGridSpec(
        num_scalar_prefetch=2,
        grid=(n_blocks,),
        in_specs=[
            pl.BlockSpec((MOE_BLOCK, PACKED_WIDTH), lambda b, be, nu: (jnp.minimum(b, nu[0] - 1), 0)),
            pl.BlockSpec((1, 1, D_MODEL, 2 * D_FF), lambda b, be, nu: (layer, be[b], 0, 0)),
            pl.BlockSpec((1, 1, 1, 2 * D_FF), lambda b, be, nu: (layer, be[b], 0, 0)),
            pl.BlockSpec((1, 1, D_FF, D_MODEL), lambda b, be, nu: (layer, be[b], 0, 0)),
            pl.BlockSpec((1, 1, 1, D_MODEL), lambda b, be, nu: (layer, be[b], 0, 0)),
        ],
        out_specs=pl.BlockSpec((MOE_BLOCK, D_MODEL), lambda b, be, nu: (b, 0)),
        scratch_shapes=[pltpu.VMEM((D_MODEL, 2 * D_FF), BF16), pltpu.VMEM((D_FF, D_MODEL), BF16)],
    )
    return pl.pallas_call(
        _expert_kernel,
        out_shape=jax.ShapeDtypeStruct((xs.shape[0], D_MODEL), F32),
        grid_spec=grid_spec,
        compiler_params=_cparams("arbitrary"),
        name="moe_experts",
    )(blk_e, nused, xs, w_gate_up, b_gate_up.reshape(depth, N_EXPERTS, 1, 2 * D_FF),
      w_down, b_down.reshape(depth, N_EXPERTS, 1, D_MODEL))


def _combine_kernel(row_ref, next_row_ref, gate_ref, x_ref, mgate_ref, gfin_ref, ys_hbm, *out_and_scratch,
                    n_ctx, final_norm):
    *out_refs, rows_scr, sems = out_and_scratch
    i = pl.program_id(0)
    n_steps = pl.num_programs(0)
    crow = _cond_row(i, TOKEN_TILE, n_ctx)
    slot = i % 2

    def row_copy(buf, src_row, k, t0, u):
        return pltpu.make_async_copy(ys_hbm.at[pl.ds(src_row, 1)],
                                     rows_scr.at[buf, k, pl.ds(t0, DMA_UNROLL)].at[pl.ds(u, 1)],
                                     sems.at[buf])

    def request(rows, buf, t0):
        for u in range(DMA_UNROLL):
            for k in range(TOP_K):
                row_copy(buf, rows[0, 0, (t0 + u) * TOP_K + k], k, t0, u).start(priority=k % 2)

    def finish(o_ref, t0):
        chunk = pl.ds(t0, DMA_UNROLL)
        g = gate_ref[chunk, :]
        y = g[:, 0:1] * rows_scr[slot, 0, chunk, :]
        for k in range(1, TOP_K):
            y = y + g[:, k:k + 1] * rows_scr[slot, k, chunk, :]
        o_ref[chunk, :] = x_ref[chunk, :] + mgate_ref[0, 0, pl.ds(crow, 1), :] * y

    def chunks(body):
        def step(tt, carry):
            body(pl.multiple_of(tt * DMA_UNROLL, DMA_UNROLL))
            return carry

        lax.fori_loop(0, TOKEN_TILE // DMA_UNROLL, step, 0)

    @pl.when(i == 0)
    def _():
        chunks(lambda t0: request(row_ref, 0, t0))

    def drain(t, carry):
        for k in range(TOP_K):
            row_copy(slot, 0, 0, 0, 0).wait()
        return carry

    lax.fori_loop(0, TOKEN_TILE, drain, 0, unroll=DMA_UNROLL)

    def consume(o_ref):
        @pl.when(i + 1 < n_steps)
        def _():
            def both(t0):
                finish(o_ref, t0)
                request(next_row_ref, 1 - slot, t0)

            chunks(both)

        @pl.when(i + 1 == n_steps)
        def _():
            chunks(functools.partial(finish, o_ref))

        if final_norm:
            x = o_ref[...]
            o_ref[...] = x * lax.rsqrt(jnp.mean(x * x, axis=-1, keepdims=True) + EPS) * gfin_ref[...]

    if len(out_refs) == 1:
        consume(out_refs[0])
    else:
        ctx_ref, lat_ref = out_refs
        is_ctx = i * TOKEN_TILE < n_ctx
        pl.when(is_ctx)(lambda: consume(ctx_ref))
        pl.when(jnp.logical_not(is_ctx))(lambda: consume(lat_ref))


def _combine(rows, gates, x, mod, layer, ys, g_final, n_ctx, final_norm):
    n_tok = x.shape[0]
    nt = n_tok // TOKEN_TILE
    ctx_tiles = n_ctx // TOKEN_TILE
    row_block = (1, 1, TOKEN_TILE * TOP_K)
    rows = rows.reshape(nt, 1, TOKEN_TILE * TOP_K)
    tile = (TOKEN_TILE, D_MODEL)
    if final_norm:
        out_shape = (jax.ShapeDtypeStruct((n_ctx, D_MODEL), F32), jax.ShapeDtypeStruct((n_tok - n_ctx, D_MODEL), F32))
        out_specs = (pl.BlockSpec(tile, lambda i: (jnp.minimum(i, ctx_tiles - 1), 0)),
                     pl.BlockSpec(tile, lambda i: (jnp.maximum(i - ctx_tiles, 0), 0)))
    else:
        out_shape = jax.ShapeDtypeStruct((n_tok, D_MODEL), F32)
        out_specs = pl.BlockSpec(tile, lambda i: (i, 0))
    return pl.pallas_call(
        functools.partial(_combine_kernel, n_ctx=n_ctx, final_norm=final_norm),
        out_shape=out_shape,
        grid=(nt,),
        in_specs=[
            pl.BlockSpec(row_block, lambda i: (i, 0, 0), memory_space=pltpu.SMEM),
            pl.BlockSpec(row_block, lambda i: (jnp.minimum(i + 1, nt - 1), 0, 0), memory_space=pltpu.SMEM),
            pl.BlockSpec((TOKEN_TILE, LANES), lambda i: (i, 0)),
            pl.BlockSpec(tile, lambda i: (i, 0)),
            pl.BlockSpec((1, 1, COND_ROWS, D_MODEL), lambda i: (layer, 5, 0, 0)),
            pl.BlockSpec((1, D_MODEL), lambda i: (0, 0)),
            pl.BlockSpec(memory_space=pl.ANY),
        ],
        out_specs=out_specs,
        scratch_shapes=[pltpu.VMEM((2, TOP_K, TOKEN_TILE, D_MODEL), F32), pltpu.SemaphoreType.DMA((2,))],
        compiler_params=_cparams("arbitrary"),
        name="moe_combine",
    )(rows, rows, gates, x, mod, g_final, ys)


def _rope_tables():
    t = np.arange(LAT_SEQ)
    row = (t // GRID_W).astype(np.float32)
    col = (t % GRID_W).astype(np.float32)
    half = HEAD_DIM // 2
    inv = jnp.asarray(ROPE_THETA, F32) ** (-jnp.arange(0, half, 2, dtype=F32) / half)
    ang = jnp.concatenate([row[:, None] * inv, col[:, None] * inv], axis=-1)
    cos, sin = jnp.cos(ang), jnp.sin(ang)
    return jnp.concatenate([cos, cos], axis=-1), jnp.concatenate([-sin, sin], axis=-1)


def _slot_rows(codes, starts):
    expert = lax.shift_right_logical(codes, SLOT_SHIFT)
    base = jnp.sum(jnp.where(expert[..., None] == jnp.arange(N_EXPERTS, dtype=I32), starts, 0), axis=-1)
    return base + (codes & (SLOT_BASE - 1))


def _routing_plan(counts_f, n_blocks):
    counts = counts_f[0, :N_EXPERTS].astype(I32)
    padded = (counts + MOE_BLOCK - 1) // MOE_BLOCK * MOE_BLOCK
    pend = jnp.cumsum(padded)
    starts = pend - padded
    nused = pend[-1] // MOE_BLOCK
    blk = jnp.minimum(jnp.arange(n_blocks, dtype=I32), nused - 1)
    blk_e = jnp.sum((pend[None, :] <= (blk * MOE_BLOCK)[:, None]).astype(I32), axis=1)
    blk_e = jnp.minimum(blk_e, N_EXPERTS - 1)
    return counts, starts.astype(I32), nused.reshape(1).astype(I32), blk_e


def _forward(x_prompt, x_sample, cache_na_k, cache_na_v, cache_gqa_k, cache_gqa_v, c, c_ctx,
             w_ada, b_ada, g_norm1, g_norm2, w_in, w_pool, pool_scale, na_rel_bias, g_q, g_k,
             w_out, w_router, b_router, w_gate_up, b_gate_up, w_down, b_down, g_final):
    depth = w_in.shape[0]
    nb_ctx, nb_lat = x_prompt.shape[0], x_sample.shape[0]
    n_ctx, n_lat = nb_ctx * CTX_SEQ, nb_lat * LAT_SEQ
    n_tok = n_ctx + n_lat
    assert x_prompt.shape[1:] == (CTX_SEQ, D_MODEL) and x_sample.shape[1:] == (LAT_SEQ, D_MODEL)
    assert n_ctx % LAT_SEQ == 0 and 1 + nb_lat <= COND_ROWS
    n_blocks = n_tok * TOP_K // MOE_BLOCK + N_EXPERTS

    x = jnp.concatenate([x_prompt.reshape(n_ctx, D_MODEL), x_sample.reshape(n_lat, D_MODEL)], axis=0)
    cond = jnp.zeros((COND_ROWS, D_MODEL), F32).at[0].set(c_ctx).at[1:1 + nb_lat].set(c)
    mod = _ada_mod(cond, w_ada, b_ada)

    cos2, sin2 = _rope_tables()
    na_bias = _na_bias_tables(na_rel_bias)
    w_in_b = w_in.astype(BF16)
    w_out_b = w_out.astype(BF16)
    w_pool_b = w_pool.astype(BF16)
    w_router_b = jnp.pad(w_router, ((0, 0), (0, 0), (0, LANES - N_EXPERTS))).astype(BF16)
    b_router_p = jnp.pad(b_router, ((0, 0), (0, LANES - N_EXPERTS)), constant_values=NEG)
    cna_k = cache_na_k.reshape(nb_lat, depth, CTX_SEQ, NA_WIDTH).astype(BF16)
    cna_v = cache_na_v.reshape(nb_lat, depth, CTX_SEQ, NA_WIDTH).astype(BF16)
    cgq_k = cache_gqa_k.reshape(nb_lat, depth, CTX_SEQ, GQA_KV_WIDTH).astype(BF16)
    cgq_v = cache_gqa_v.reshape(nb_lat, depth, CTX_SEQ, GQA_KV_WIDTH).astype(BF16)

    new_kv = []
    for l in range(depth):
        u, kv_f, proj_b = _inproj(x, mod, l, g_norm1[l][None], w_in_b, g_q[l][None], g_k[l][None],
                                  cos2, sin2, n_ctx)
        new_kv.append(kv_f[:n_ctx])
        ya = _pool(u, w_pool_b[l], pool_scale[l][None], n_ctx)
        yb = _na_attention(proj_b, cna_k, cna_v, na_bias, l, n_ctx)
        yc = _gqa_attention(proj_b, cgq_k, cgq_v, l, n_ctx)
        x, h, logits = _outproj(x, ya, yb, yc, w_out_b, mod, l, g_norm2[l][None],
                                w_router_b[l], b_router_p[l][None], n_ctx)
        codes, gates, counts_f = _route(logits)
        counts, starts, nused, blk_e = _routing_plan(counts_f, n_blocks)
        rows = _slot_rows(codes[:, :TOP_K], starts)
        xs = _dispatch(counts, starts, nused, rows, h, n_blocks)
        ys = _experts(blk_e, nused, xs, w_gate_up, b_gate_up, w_down, b_down, l)
        x = _combine(rows, gates, x, mod, l, ys, g_final[None], n_ctx, final_norm=(l == depth - 1))

    y_ctx, y_lat = x
    y_prompt = y_ctx.reshape(nb_ctx, CTX_SEQ, D_MODEL)
    y_sample = y_lat.reshape(nb_lat, LAT_SEQ, D_MODEL)

    def stack(col, width):
        per_layer = [p[:, col:col + width].reshape(nb_ctx, CTX_SEQ, width // HEAD_DIM, HEAD_DIM) for p in new_kv]
        return jnp.stack(per_layer, axis=1)

    return (y_prompt, y_sample, stack(KVF_KN, NA_WIDTH), stack(KVF_VN, NA_WIDTH),
            stack(KVF_KG, GQA_KV_WIDTH), stack(KVF_VG, GQA_KV_WIDTH))


def kernel(x_prompt, x_sample, cache_na_k, cache_na_v, cache_gqa_k, cache_gqa_v, c, c_ctx, w_ada, b_ada,
           g_norm1, g_norm2, w_in, w_pool, pool_scale, na_rel_bias, g_q, g_k, w_out, w_router, b_router,
           w_gate_up, b_gate_up, w_down, b_down, g_final):
    return _forward(x_prompt, x_sample, cache_na_k, cache_na_v, cache_gqa_k, cache_gqa_v, c, c_ctx,
                    w_ada, b_ada, g_norm1, g_norm2, w_in, w_pool, pool_scale, na_rel_bias, g_q, g_k,
                    w_out, w_router, b_router, w_gate_up, b_gate_up, w_down, b_down, g_final)
```

```python
import functools
import math

import numpy as np
import jax
import jax.numpy as jnp
from jax import lax
from jax.experimental import pallas as pl
from jax.experimental.pallas import tpu as pltpu

F32 = jnp.float32
BF16 = jnp.bfloat16
I32 = jnp.int32

D_MODEL = 2048
HEAD_DIM = 128
CTX_SEQ = 256
LAT_SEQ = 2048
GRID_W = 64
GRID_ROWS = LAT_SEQ // GRID_W
POOL_WINDOWS = (2, 4, 8, 16)
POOL_GROUP = 128
POOL_WIDTH = 512
NA_HEADS = 4
NA_WIDTH = 512
NA_ROWS = 8
NA_COLS = 16
GQA_HEADS = 8
GQA_KV_HEADS = 2
GQA_GROUP = GQA_HEADS // GQA_KV_HEADS
GQA_WIDTH = 1024
GQA_KV_WIDTH = 256
IN_WIDTH = 3584
N_EXPERTS = 32
TOP_K = 4
D_FF = 512
SWIGLU_LIMIT = 7.0
SWIGLU_ALPHA = 1.702
ROPE_THETA = 10000.0
EPS = 1e-6
NEG = -1e30

COL_U, COL_QN, COL_KN, COL_VN, COL_QG, COL_KG, COL_VG = 0, 512, 1024, 1536, 2048, 3072, 3328
ACT_QG, ACT_QN, ACT_KN, ACT_VN, ACT_KG, ACT_VG, ACT_WIDTH = 0, 1024, 1536, 2048, 2560, 2816, 3072
KVF_KN, KVF_VN, KVF_KG, KVF_VG, KVF_WIDTH = 0, 512, 1024, 1280, 1536

LANES = 128
COND_ROWS = 16
ATT_TILE = 256
NA_TILE_ROWS = ATT_TILE // GRID_W
NA_WIN_ROWS = NA_ROWS + NA_TILE_ROWS - 1
NA_WIN = NA_WIN_ROWS * GRID_W
INPROJ_TM = 512
OUTPROJ_TM = 512
ROUTE_TM = 512
MOE_BLOCK = 512
TOKEN_TILE = 256
DISPATCH_TILE = 512
SLOT_SHIFT = 16
SLOT_BASE = 1 << SLOT_SHIFT
DMA_UNROLL = 8
VMEM_LIMIT = 56 * 1024 * 1024


def _cparams(*sem):
    return pltpu.CompilerParams(dimension_semantics=sem, vmem_limit_bytes=VMEM_LIMIT)


def _cond_row(tile, tile_rows, n_ctx):
    row0 = tile * tile_rows
    return jnp.where(row0 < n_ctx, 0, 1 + (row0 - n_ctx) // LAT_SEQ)


def _head(h):
    return slice(h * HEAD_DIM, (h + 1) * HEAD_DIM)


U32 = jnp.uint32
PACKED_WIDTH = D_MODEL // 2


def _pack_bf16_pairs(x):
    half = x.shape[1] // 2
    hi = pltpu.bitcast(x[:, :half].astype(BF16).astype(F32), U32)
    lo = pltpu.bitcast(x[:, half:].astype(BF16).astype(F32), U32)
    return hi | (lo >> 16)


def _unpack_bf16_pairs(p):
    first = pltpu.bitcast(p & jnp.uint32(0xFFFF0000), F32).astype(BF16)
    second = pltpu.bitcast(p << 16, F32).astype(BF16)
    return first, second


def _ada_kernel(c_ref, w_ref, b_ref, o_ref):
    c = c_ref[...]
    s = (c * jax.nn.sigmoid(c)).astype(BF16)
    o_ref[0, 0] = jnp.dot(s, w_ref[0].astype(BF16), preferred_element_type=F32) + b_ref[0]


def _ada_mod(cond, w_ada, b_ada):
    depth = w_ada.shape[0]
    tn = 1024
    per = D_MODEL // tn
    return pl.pallas_call(
        _ada_kernel,
        out_shape=jax.ShapeDtypeStruct((depth, 6, COND_ROWS, D_MODEL), F32),
        grid=(depth, 6 * per),
        in_specs=[
            pl.BlockSpec((COND_ROWS, D_MODEL), lambda l, j: (0, 0)),
            pl.BlockSpec((1, D_MODEL, tn), lambda l, j: (l, 0, j)),
            pl.BlockSpec((1, 1, tn), lambda l, j: (l, 0, j)),
        ],
        out_specs=pl.BlockSpec((1, 1, COND_ROWS, tn), lambda l, j: (l, j // per, 0, j % per)),
        compiler_params=_cparams("arbitrary", "arbitrary"),
        name="ada_mod",
    )(cond, w_ada, b_ada.reshape(depth, 1, 6 * D_MODEL))


def _head_rms(x, g):
    return x * lax.rsqrt(jnp.mean(x * x, axis=-1, keepdims=True) + EPS) * g


NORM_CHUNKS = 8


def _inproj_kernel(x0_ref, xn_ref, shift_ref, scale_ref, g1_ref, w_ref, gq_ref, gk_ref, cos_ref, sin_ref,
                   u_ref, kv_ref, act_ref, h_even, h_odd, *, n_ctx):
    i = pl.program_id(0)
    nxt = jnp.minimum(i + 1, pl.num_programs(0) - 1)
    is_lat = i * INPROJ_TM >= n_ctx
    cos = jnp.where(is_lat, cos_ref[...], 1.0)
    sin = jnp.where(is_lat, sin_ref[...], 0.0)

    def normalize(x_ref, tile, h_ref, rows):
        crow = _cond_row(tile, INPROJ_TM, n_ctx)
        x = x_ref[rows, :]
        y = x * lax.rsqrt(jnp.mean(x * x, axis=-1, keepdims=True) + EPS) * g1_ref[...]
        h_ref[rows, :] = (y * (1 + scale_ref[0, 0, pl.ds(crow, 1), :])
                          + shift_ref[0, 0, pl.ds(crow, 1), :]).astype(BF16)

    def normed_rope(xh, g_ref):
        xh = _head_rms(xh, g_ref[...])
        return xh * cos + pltpu.roll(xh, HEAD_DIM // 2, 1) * sin

    def project(h_ref, h_next_ref):
        chunk = INPROJ_TM // NORM_CHUNKS
        steps = iter(range(NORM_CHUNKS))

        def proj(col, width):
            acc = jnp.dot(h_ref[...], w_ref[0, :, col:col + width], preferred_element_type=F32)
            c = next(steps)
            normalize(xn_ref, nxt, h_next_ref, slice(c * chunk, (c + 1) * chunk))
            return acc

        for half in range(GQA_KV_HEADS):
            q = proj(COL_QG + half * GQA_GROUP * HEAD_DIM, GQA_GROUP * HEAD_DIM)
            for g in range(GQA_GROUP):
                col = ACT_QG + (half * GQA_GROUP + g) * HEAD_DIM
                act_ref[:, col:col + HEAD_DIM] = normed_rope(q[:, _head(g)], gq_ref).astype(BF16)
        kg = proj(COL_KG, GQA_KV_WIDTH)
        for kh in range(GQA_KV_HEADS):
            v = normed_rope(kg[:, _head(kh)], gk_ref)
            kv_ref[:, KVF_KG + kh * HEAD_DIM:KVF_KG + (kh + 1) * HEAD_DIM] = v
            act_ref[:, ACT_KG + kh * HEAD_DIM:ACT_KG + (kh + 1) * HEAD_DIM] = v.astype(BF16)
        vg = proj(COL_VG, GQA_KV_WIDTH)
        kv_ref[:, KVF_VG:KVF_VG + GQA_KV_WIDTH] = vg
        act_ref[:, ACT_VG:ACT_VG + GQA_KV_WIDTH] = vg.astype(BF16)
        for col, kv_col, act_col in ((COL_KN, KVF_KN, ACT_KN), (COL_VN, KVF_VN, ACT_VN)):
            v = proj(col, NA_WIDTH)
            kv_ref[:, kv_col:kv_col + NA_WIDTH] = v
            act_ref[:, act_col:act_col + NA_WIDTH] = v.astype(BF16)
        act_ref[:, ACT_QN:ACT_QN + NA_WIDTH] = proj(COL_QN, NA_WIDTH).astype(BF16)
        u_ref[...] = proj(COL_U, POOL_WIDTH)

    @pl.when(i == 0)
    def _():
        normalize(x0_ref, 0, h_even, slice(None))

    @pl.when(i % 2 == 0)
    def _():
        project(h_even, h_odd)

    @pl.when(i % 2 == 1)
    def _():
        project(h_odd, h_even)


def _inproj(x, mod, layer, g1, w_in_b, g_q, g_k, cos2, sin2, n_ctx):
    n_tok = x.shape[0]
    tm = INPROJ_TM
    lat_tiles = LAT_SEQ // tm
    ctx_tiles = n_ctx // tm

    def rope_map(i):
        return (jnp.maximum(i - ctx_tiles, 0) % lat_tiles, 0)

    def rows(width):
        return pl.BlockSpec((tm, width), lambda i: (i, 0))

    return pl.pallas_call(
        functools.partial(_inproj_kernel, n_ctx=n_ctx),
        out_shape=(jax.ShapeDtypeStruct((n_tok, POOL_WIDTH), F32),
                   jax.ShapeDtypeStruct((n_tok, KVF_WIDTH), F32),
                   jax.ShapeDtypeStruct((n_tok, ACT_WIDTH), BF16)),
        grid=(n_tok // tm,),
        in_specs=[
            pl.BlockSpec((tm, D_MODEL), lambda i: (0, 0)),
            pl.BlockSpec((tm, D_MODEL), lambda i: (jnp.minimum(i + 1, n_tok // tm - 1), 0)),
            pl.BlockSpec((1, 1, COND_ROWS, D_MODEL), lambda i: (layer, 0, 0, 0)),
            pl.BlockSpec((1, 1, COND_ROWS, D_MODEL), lambda i: (layer, 1, 0, 0)),
            pl.BlockSpec((1, D_MODEL), lambda i: (0, 0)),
            pl.BlockSpec((1, D_MODEL, IN_WIDTH), lambda i: (layer, 0, 0), pipeline_mode=pl.Buffered(1)),
            pl.BlockSpec((1, HEAD_DIM), lambda i: (0, 0)),
            pl.BlockSpec((1, HEAD_DIM), lambda i: (0, 0)),
            pl.BlockSpec((tm, HEAD_DIM), rope_map),
            pl.BlockSpec((tm, HEAD_DIM), rope_map),
        ],
        out_specs=(rows(POOL_WIDTH), rows(KVF_WIDTH), rows(ACT_WIDTH)),
        scratch_shapes=[pltpu.VMEM((tm, D_MODEL), BF16), pltpu.VMEM((tm, D_MODEL), BF16)],
        compiler_params=_cparams("arbitrary"),
        name="norm_inproj",
    )(x, x, mod, mod, g1, w_in_b, g_q, g_k, cos2, sin2)


POOL_HALO = 8


def _pool_kernel(up_ref, uc_ref, un_ref, wp_ref, ps_ref, o_ref, pad_scr, *, n_ctx):
    i = pl.program_id(0)
    row0 = i * ATT_TILE
    is_ctx = row0 < n_ctx
    t0 = jnp.where(is_ctx, 0, (row0 - n_ctx) % LAT_SEQ)
    seq = jnp.where(is_ctx, CTX_SEQ, LAT_SEQ)
    first = t0 == 0
    last = t0 + ATT_TILE == seq
    zeros = jnp.zeros((POOL_HALO, POOL_WIDTH), F32)
    pad_scr[0:POOL_HALO, :] = jnp.where(first, zeros, up_ref[ATT_TILE - POOL_HALO:ATT_TILE, :])
    pad_scr[POOL_HALO:POOL_HALO + ATT_TILE, :] = uc_ref[...]
    pad_scr[POOL_HALO + ATT_TILE:, :] = jnp.where(last, zeros, un_ref[0:POOL_HALO, :])
    t = t0 + lax.broadcasted_iota(I32, (ATT_TILE, 1), 0)
    for g, w in enumerate(POOL_WINDOWS):
        cols = slice(g * POOL_GROUP, (g + 1) * POOL_GROUP)
        acc = pad_scr[pl.ds(POOL_HALO - w // 2, ATT_TILE), cols]
        for d in range(-w // 2 + 1, w // 2):
            acc = acc + pad_scr[pl.ds(POOL_HALO + d, ATT_TILE), cols]
        cnt = (jnp.minimum(t + w // 2, seq) - jnp.maximum(t - w // 2, 0)).astype(F32)
        pooled = acc / cnt - uc_ref[:, cols]
        y = jnp.dot(pooled.astype(BF16), wp_ref[g], preferred_element_type=F32)
        o_ref[:, cols] = (y * ps_ref[:, cols]).astype(BF16)


def _pool(proj_f, w_pool_b, pool_scale, n_ctx):
    n_tok = proj_f.shape[0]
    nt = n_tok // ATT_TILE
    return pl.pallas_call(
        functools.partial(_pool_kernel, n_ctx=n_ctx),
        out_shape=jax.ShapeDtypeStruct((n_tok, POOL_WIDTH), BF16),
        grid=(nt,),
        in_specs=[
            pl.BlockSpec((ATT_TILE, POOL_WIDTH), lambda i: (jnp.maximum(i - 1, 0), 0)),
            pl.BlockSpec((ATT_TILE, POOL_WIDTH), lambda i: (i, 0)),
            pl.BlockSpec((ATT_TILE, POOL_WIDTH), lambda i: (jnp.minimum(i + 1, nt - 1), 0)),
            pl.BlockSpec((len(POOL_WINDOWS), POOL_GROUP, POOL_GROUP), lambda i: (0, 0, 0)),
            pl.BlockSpec((1, POOL_WIDTH), lambda i: (0, 0)),
        ],
        out_specs=pl.BlockSpec((ATT_TILE, POOL_WIDTH), lambda i: (i, 0)),
        scratch_shapes=[pltpu.VMEM((ATT_TILE + 2 * POOL_HALO, POOL_WIDTH), F32)],
        compiler_params=_cparams("arbitrary"),
        name="pool_mixer",
    )(proj_f, proj_f, proj_f, w_pool_b, pool_scale)


_NT_DIMS = (((1,), (1,)), ((), ()))
ATT_SCALE = HEAD_DIM ** -0.5
EXP2_SCALE = ATT_SCALE * math.log2(math.e)


def _scores(q, k):
    return lax.dot_general(q, k, _NT_DIMS, preferred_element_type=F32)


def _attend_one(q, k, v):
    s = _scores(q, k)
    p = jnp.exp2((s - jnp.max(s, axis=-1, keepdims=True)) * EXP2_SCALE)
    o = jnp.dot(p.astype(BF16), v, preferred_element_type=F32)
    return o / jnp.sum(p, axis=-1, keepdims=True)


def _attend_two(s1, v1, s2, v2):
    m = jnp.maximum(jnp.max(s1, axis=-1, keepdims=True), jnp.max(s2, axis=-1, keepdims=True))
    p1 = jnp.exp2((s1 - m) * EXP2_SCALE)
    p2 = jnp.exp2((s2 - m) * EXP2_SCALE)
    o = (jnp.dot(p1.astype(BF16), v1, preferred_element_type=F32)
         + jnp.dot(p2.astype(BF16), v2, preferred_element_type=F32))
    return o / (jnp.sum(p1, axis=-1, keepdims=True) + jnp.sum(p2, axis=-1, keepdims=True))


def _na_kernel(q_ref, kc_ref, vc_ref, kl_ref, vl_ref, ck_ref, cv_ref, bias_ref, o_ref, *, n_ctx_steps):
    i = pl.program_id(0)

    @pl.when(i < n_ctx_steps)
    def _():
        for h in range(NA_HEADS):
            o = _attend_one(q_ref[:, _head(h)], kc_ref[:, _head(h)], vc_ref[:, _head(h)])
            o_ref[:, _head(h)] = o.astype(BF16)

    @pl.when(i >= n_ctx_steps)
    def _():
        tile = (i - n_ctx_steps) % (LAT_SEQ // ATT_TILE)
        win_row = jnp.clip(tile * NA_TILE_ROWS - NA_ROWS // 2, 0, GRID_ROWS - NA_WIN_ROWS)
        key0 = pl.multiple_of(win_row * GRID_W, GRID_W)
        def scores(h):
            q = q_ref[:, _head(h)]
            return (_scores(q, kl_ref[pl.ds(key0, NA_WIN), _head(h)]) + bias_ref[0, 0, h],
                    _scores(q, ck_ref[0, 0, :, _head(h)]))

        nxt = scores(0)
        for h in range(NA_HEADS):
            s_lat, s_ctx = nxt
            if h + 1 < NA_HEADS:
                nxt = scores(h + 1)
            o = _attend_two(s_lat, vl_ref[pl.ds(key0, NA_WIN), _head(h)],
                            s_ctx, cv_ref[0, 0, :, _head(h)])
            o_ref[:, _head(h)] = o.astype(BF16)


def _na_bias_tables(rel_bias):
    depth = rel_bias.shape[0]
    half = NA_ROWS // 2
    r0 = np.array([0, half, GRID_ROWS - NA_TILE_ROWS])[:, None, None]
    win = np.clip(r0 - half, 0, GRID_ROWS - NA_WIN_ROWS)
    qr = r0 + np.arange(NA_TILE_ROWS)[None, :, None]
    kr = win + np.arange(NA_WIN_ROWS)[None, None, :]
    rs = np.clip(qr - half, 0, GRID_ROWS - NA_ROWS)
    row_ok = (kr >= rs) & (kr < rs + NA_ROWS)
    dr = np.clip(kr - qr + NA_ROWS - 1, 0, 2 * NA_ROWS - 2)
    qc = np.arange(GRID_W)[:, None]
    kc = np.arange(GRID_W)[None, :]
    cs = np.clip(qc - NA_COLS // 2, 0, GRID_W - NA_COLS)
    col_ok = (kc >= cs) & (kc < cs + NA_COLS)
    dc = np.clip(kc - qc + NA_COLS - 1, 0, 2 * NA_COLS - 2)
    sel_r = (dr[..., None] == np.arange(2 * NA_ROWS - 1)).astype(np.float32)
    sel_c = (dc[..., None] == np.arange(2 * NA_COLS - 1)).astype(np.float32)
    vals = jnp.einsum("sqka,lhab,xyb->lshqxky", sel_r, rel_bias, sel_c, precision=lax.Precision.HIGHEST)
    ok = row_ok[:, :, None, :, None] & col_ok[None, None, :, None, :]
    table = jnp.where(ok[None, :, None], vals / ATT_SCALE, NEG)
    return table.reshape(depth, 3, NA_HEADS, ATT_TILE, NA_WIN).astype(F32)


def _na_attention(proj_b, cache_k, cache_v, bias, layer, n_ctx):
    n_tok = proj_b.shape[0]
    n_ctx_steps = n_ctx // ATT_TILE
    lat_tiles = LAT_SEQ // ATT_TILE
    lat_blk0 = n_ctx // LAT_SEQ
    wq = NA_WIDTH

    def ctx_blk(i):
        return jnp.minimum(i, n_ctx_steps - 1)

    def req(i):
        return jnp.maximum(i - n_ctx_steps, 0) // lat_tiles

    def kind(i):
        tile = jnp.maximum(i - n_ctx_steps, 0) % lat_tiles
        return jnp.where(tile == 0, 0, jnp.where(tile == lat_tiles - 1, 2, 1))

    return pl.pallas_call(
        functools.partial(_na_kernel, n_ctx_steps=n_ctx_steps),
        out_shape=jax.ShapeDtypeStruct((n_tok, NA_WIDTH), BF16),
        grid=(n_tok // ATT_TILE,),
        in_specs=[
            pl.BlockSpec((ATT_TILE, wq), lambda i: (i, ACT_QN // wq)),
            pl.BlockSpec((ATT_TILE, wq), lambda i: (ctx_blk(i), ACT_KN // wq)),
            pl.BlockSpec((ATT_TILE, wq), lambda i: (ctx_blk(i), ACT_VN // wq)),
            pl.BlockSpec((LAT_SEQ, wq), lambda i: (lat_blk0 + req(i), ACT_KN // wq)),
            pl.BlockSpec((LAT_SEQ, wq), lambda i: (lat_blk0 + req(i), ACT_VN // wq)),
            pl.BlockSpec((1, 1, CTX_SEQ, wq), lambda i: (req(i), layer, 0, 0)),
            pl.BlockSpec((1, 1, CTX_SEQ, wq), lambda i: (req(i), layer, 0, 0)),
            pl.BlockSpec((1, 1, NA_HEADS, ATT_TILE, NA_WIN), lambda i: (layer, kind(i), 0, 0, 0)),
        ],
        out_specs=pl.BlockSpec((ATT_TILE, wq), lambda i: (i, 0)),
        compiler_params=_cparams("arbitrary"),
        name="na_attention",
    )(proj_b, proj_b, proj_b, proj_b, proj_b, cache_k, cache_v, bias)


def _gqa_kernel(q_ref, kc_ref, vc_ref, kl_ref, vl_ref, ck_ref, cv_ref, o_ref, *, n_ctx_steps):
    i = pl.program_id(0)

    def kv_head(h):
        return _head(h // GQA_GROUP)

    @pl.when(i < n_ctx_steps)
    def _():
        for h in range(GQA_HEADS):
            o = _attend_one(q_ref[:, _head(h)], kc_ref[:, kv_head(h)], vc_ref[:, kv_head(h)])
            o_ref[:, _head(h)] = o.astype(BF16)

    @pl.when(i >= n_ctx_steps)
    def _():
        def scores(h):
            q = q_ref[:, _head(h)]
            return _scores(q, kl_ref[:, kv_head(h)]), _scores(q, ck_ref[0, 0, :, kv_head(h)])

        nxt = scores(0)
        for h in range(GQA_HEADS):
            s_lat, s_ctx = nxt
            if h + 1 < GQA_HEADS:
                nxt = scores(h + 1)
            o = _attend_two(s_lat, vl_ref[:, kv_head(h)], s_ctx, cv_ref[0, 0, :, kv_head(h)])
            o_ref[:, _head(h)] = o.astype(BF16)


def _gqa_attention(proj_b, cache_k, cache_v, layer, n_ctx):
    n_tok = proj_b.shape[0]
    n_ctx_steps = n_ctx // ATT_TILE
    lat_tiles = LAT_SEQ // ATT_TILE
    lat_blk0 = n_ctx // LAT_SEQ
    wkv = GQA_KV_WIDTH

    def ctx_blk(i):
        return jnp.minimum(i, n_ctx_steps - 1)

    def req(i):
        return jnp.maximum(i - n_ctx_steps, 0) // lat_tiles

    return pl.pallas_call(
        functools.partial(_gqa_kernel, n_ctx_steps=n_ctx_steps),
        out_shape=jax.ShapeDtypeStruct((n_tok, GQA_WIDTH), BF16),
        grid=(n_tok // ATT_TILE,),
        in_specs=[
            pl.BlockSpec((ATT_TILE, GQA_WIDTH), lambda i: (i, ACT_QG // GQA_WIDTH)),
            pl.BlockSpec((ATT_TILE, wkv), lambda i: (ctx_blk(i), ACT_KG // wkv)),
            pl.BlockSpec((ATT_TILE, wkv), lambda i: (ctx_blk(i), ACT_VG // wkv)),
            pl.BlockSpec((LAT_SEQ, wkv), lambda i: (lat_blk0 + req(i), ACT_KG // wkv)),
            pl.BlockSpec((LAT_SEQ, wkv), lambda i: (lat_blk0 + req(i), ACT_VG // wkv)),
            pl.BlockSpec((1, 1, CTX_SEQ, wkv), lambda i: (req(i), layer, 0, 0)),
            pl.BlockSpec((1, 1, CTX_SEQ, wkv), lambda i: (req(i), layer, 0, 0)),
        ],
        out_specs=pl.BlockSpec((ATT_TILE, GQA_WIDTH), lambda i: (i, 0)),
        compiler_params=_cparams("arbitrary"),
        name="gqa_attention",
    )(proj_b, proj_b, proj_b, proj_b, proj_b, cache_k, cache_v)


def _outproj_kernel(x_ref, ya_ref, yb_ref, yc_ref, wo_ref, gate_ref, shift_ref, scale_ref, g2_ref,
                    wr_ref, br_ref, xo_ref, h_ref, lg_ref, *, n_ctx):
    crow = _cond_row(pl.program_id(0), OUTPROJ_TM, n_ctx)
    mix = (jnp.dot(ya_ref[...], wo_ref[0, 0:POOL_WIDTH, :], preferred_element_type=F32)
           + jnp.dot(yb_ref[...], wo_ref[0, POOL_WIDTH:POOL_WIDTH + NA_WIDTH, :], preferred_element_type=F32)
           + jnp.dot(yc_ref[...], wo_ref[0, POOL_WIDTH + NA_WIDTH:, :], preferred_element_type=F32))
    x = x_ref[...] + gate_ref[0, 0, pl.ds(crow, 1), :] * mix
    xo_ref[...] = x
    y = x * lax.rsqrt(jnp.mean(x * x, axis=-1, keepdims=True) + EPS) * g2_ref[...]
    h = y * (1 + scale_ref[0, 0, pl.ds(crow, 1), :]) + shift_ref[0, 0, pl.ds(crow, 1), :]
    h_ref[...] = _pack_bf16_pairs(h)
    lg_ref[...] = jnp.dot(h.astype(BF16), wr_ref[...], preferred_element_type=F32) + br_ref[...]


def _outproj(x, ya, yb, yc, w_out_b, mod, layer, g2, w_router_b, b_router_p, n_ctx):
    n_tok = x.shape[0]
    tm = OUTPROJ_TM

    def mod_spec(chunk):
        return pl.BlockSpec((1, 1, COND_ROWS, D_MODEL), lambda i: (layer, chunk, 0, 0))

    return pl.pallas_call(
        functools.partial(_outproj_kernel, n_ctx=n_ctx),
        out_shape=(jax.ShapeDtypeStruct((n_tok, D_MODEL), F32),
                   jax.ShapeDtypeStruct((n_tok, PACKED_WIDTH), U32),
                   jax.ShapeDtypeStruct((n_tok, LANES), F32)),
        grid=(n_tok // tm,),
        in_specs=[
            pl.BlockSpec((tm, D_MODEL), lambda i: (i, 0)),
            pl.BlockSpec((tm, POOL_WIDTH), lambda i: (i, 0)),
            pl.BlockSpec((tm, NA_WIDTH), lambda i: (i, 0)),
            pl.BlockSpec((tm, GQA_WIDTH), lambda i: (i, 0)),
            pl.BlockSpec((1, D_MODEL, D_MODEL), lambda i: (layer, 0, 0)),
            mod_spec(2), mod_spec(3), mod_spec(4),
            pl.BlockSpec((1, D_MODEL), lambda i: (0, 0)),
            pl.BlockSpec((D_MODEL, LANES), lambda i: (0, 0)),
            pl.BlockSpec((1, LANES), lambda i: (0, 0)),
        ],
        out_specs=(pl.BlockSpec((tm, D_MODEL), lambda i: (i, 0)),
                   pl.BlockSpec((tm, PACKED_WIDTH), lambda i: (i, 0)),
                   pl.BlockSpec((tm, LANES), lambda i: (i, 0))),
        compiler_params=_cparams("arbitrary"),
        name="outproj_norm_router",
    )(x, ya, yb, yc, w_out_b, mod, mod, mod, g2, w_router_b, b_router_p)


def _route_kernel(lg_ref, code_ref, gate_ref, cnt_ref, carry_scr):
    i = pl.program_id(0)

    @pl.when(i == 0)
    def _():
        carry_scr[...] = jnp.zeros_like(carry_scr)

    tm = lg_ref.shape[0]
    l = lg_ref[...]
    lane = lax.broadcasted_iota(I32, (tm, LANES), 1)
    sel = jnp.zeros((tm, LANES), F32)
    vals, idxs = [], []
    for _ in range(TOP_K):
        m = jnp.max(l, axis=-1, keepdims=True)
        ik = jnp.min(jnp.where(l == m, lane, LANES), axis=-1, keepdims=True)
        hit = lane == ik
        sel = jnp.where(hit, 1.0, sel)
        l = jnp.where(hit, -jnp.inf, l)
        vals.append(m)
        idxs.append(ik)
    exps = [jnp.exp(v - vals[0]) for v in vals]
    tot = exps[0] + exps[1] + exps[2] + exps[3]
    r = lax.broadcasted_iota(I32, (tm, tm), 0)
    c = lax.broadcasted_iota(I32, (tm, tm), 1)
    tri = jnp.where(c < r, 1.0, 0.0).astype(BF16)
    before = jnp.dot(tri, sel.astype(BF16), preferred_element_type=F32) + carry_scr[...]
    code_out = jnp.zeros((tm, LANES), I32)
    gate_out = jnp.zeros((tm, LANES), F32)
    for k in range(TOP_K):
        pk = jnp.sum(jnp.where(lane == idxs[k], before, 0.0), axis=-1, keepdims=True)
        code_out = jnp.where(lane == k, idxs[k] * SLOT_BASE + pk.astype(I32), code_out)
        gate_out = jnp.where(lane == k, exps[k] / tot, gate_out)
    code_ref[...] = code_out
    gate_ref[...] = gate_out
    carry_scr[...] = carry_scr[...] + jnp.sum(sel, axis=0, keepdims=True)
    cnt_ref[...] = jnp.broadcast_to(carry_scr[...], cnt_ref.shape)


def _route(logits):
    n_tok = logits.shape[0]
    assert n_tok <= SLOT_BASE
    tm = ROUTE_TM
    tile = pl.BlockSpec((tm, LANES), lambda i: (i, 0))
    return pl.pallas_call(
        _route_kernel,
        out_shape=(jax.ShapeDtypeStruct((n_tok, LANES), I32),
                   jax.ShapeDtypeStruct((n_tok, LANES), F32),
                   jax.ShapeDtypeStruct((8, LANES), F32)),
        grid=(n_tok // tm,),
        in_specs=[tile],
        out_specs=(tile, tile, pl.BlockSpec((8, LANES), lambda i: (0, 0))),
        scratch_shapes=[pltpu.VMEM((1, LANES), F32)],
        compiler_params=_cparams("arbitrary"),
        name="route_topk",
    )(logits)


ZERO_ROWS = 64


def _dispatch_kernel(cnt_ref, start_ref, nused_ref, row_ref, h_ref, xs_hbm, zero_scr, sem, *, n_blocks):
    i = pl.program_id(0)
    last = pl.num_programs(0) - 1

    def row_copy(src_row, dst_row):
        return pltpu.make_async_copy(h_ref.at[pl.ds(src_row, 1)], xs_hbm.at[pl.ds(dst_row, 1)], sem)

    def zero_row_copy(dst_row):
        return pltpu.make_async_copy(zero_scr.at[pl.ds(0, 1)], xs_hbm.at[pl.ds(dst_row, 1)], sem)

    def zero_chunk_copy(dst_row):
        return pltpu.make_async_copy(zero_scr, xs_hbm.at[pl.ds(dst_row, ZERO_ROWS)], sem)

    def issue(tt, carry):
        t0 = pl.multiple_of(tt * DMA_UNROLL, DMA_UNROLL)
        for u in range(DMA_UNROLL):
            for k in range(TOP_K):
                dst = row_ref[0, 0, (t0 + u) * TOP_K + k]
                pltpu.make_async_copy(h_ref.at[pl.ds(t0, DMA_UNROLL)].at[pl.ds(u, 1)],
                                      xs_hbm.at[pl.ds(dst, 1)], sem).start(priority=k % 2)
        return carry

    lax.fori_loop(0, DISPATCH_TILE // DMA_UNROLL, issue, 0)

    def drain(t, carry):
        for k in range(TOP_K):
            row_copy(0, 0).wait()
        return carry

    lax.fori_loop(0, DISPATCH_TILE, drain, 0, unroll=DMA_UNROLL)

    @pl.when(i == last)
    def _():
        zero_scr[...] = jnp.zeros_like(zero_scr)

        def per_expert(e, carry):
            cnt = cnt_ref[e]
            padded = (cnt + MOE_BLOCK - 1) // MOE_BLOCK * MOE_BLOCK
            base = start_ref[e]

            def z_issue(p, c):
                zero_row_copy(base + p).start()
                return c

            def z_drain(p, c):
                zero_row_copy(0).wait()
                return c

            lax.fori_loop(cnt, padded, z_issue, 0)
            lax.fori_loop(cnt, padded, z_drain, 0)
            return carry

        lax.fori_loop(0, N_EXPERTS, per_expert, 0)

        chunks = MOE_BLOCK // ZERO_ROWS

        def t_issue(c, carry):
            zero_chunk_copy(c * ZERO_ROWS).start()
            return carry

        def t_drain(c, carry):
            zero_chunk_copy(0).wait()
            return carry

        lax.fori_loop(nused_ref[0] * chunks, n_blocks * chunks, t_issue, 0)
        lax.fori_loop(nused_ref[0] * chunks, n_blocks * chunks, t_drain, 0)


def _dispatch(counts, starts, nused, rows, h, n_blocks):
    n_tok = h.shape[0]
    nt = n_tok // DISPATCH_TILE
    grid_spec = pltpu.PrefetchScalarGridSpec(
        num_scalar_prefetch=3,
        grid=(nt,),
        in_specs=[
            pl.BlockSpec((1, 1, DISPATCH_TILE * TOP_K), lambda i, *_: (i, 0, 0), memory_space=pltpu.SMEM),
            pl.BlockSpec((DISPATCH_TILE, PACKED_WIDTH), lambda i, *_: (i, 0)),
        ],
        out_specs=pl.BlockSpec(memory_space=pl.ANY),
        scratch_shapes=[pltpu.VMEM((ZERO_ROWS, PACKED_WIDTH), U32), pltpu.SemaphoreType.DMA],
    )
    return pl.pallas_call(
        functools.partial(_dispatch_kernel, n_blocks=n_blocks),
        out_shape=jax.ShapeDtypeStruct((n_blocks * MOE_BLOCK, PACKED_WIDTH), U32),
        grid_spec=grid_spec,
        compiler_params=_cparams("arbitrary"),
        name="moe_dispatch",
    )(counts, starts, nused, rows.reshape(nt, 1, DISPATCH_TILE * TOP_K), h)


def _expert_kernel(blk_e_ref, next_e_ref, blk_rows_ref, nused_ref, xs_ref, bgu_ref, bd_ref, wgu_hbm, wd_hbm, ys_ref,
                   wgu_in, wd_in, wgu_scr, wd_scr, slot_ref, sems, *, layer):
    b = pl.program_id(0)
    e = blk_e_ref[b]
    prev = blk_e_ref[jnp.maximum(b - 1, 0)]

    def fetch(expert, slot):
        return (pltpu.make_async_copy(wgu_hbm.at[layer, expert], wgu_in.at[slot], sems.at[0, slot]),
                pltpu.make_async_copy(wd_hbm.at[layer, expert], wd_in.at[slot], sems.at[1, slot]))

    @pl.when(b == 0)
    def _():
        slot_ref[0] = 0
        for copy in fetch(e, 0):
            copy.start()

    @pl.when((b == 0) | (e != prev))
    def _():
        slot = slot_ref[0]
        for copy in fetch(e, slot):
            copy.wait()
        wgu_scr[...] = wgu_in[slot].astype(BF16)
        wd_scr[...] = wd_in[slot].astype(BF16)
        nxt = next_e_ref[b]

        @pl.when(nxt != e)
        def _():
            for copy in fetch(nxt, 1 - slot):
                copy.start()

        slot_ref[0] = 1 - slot

    def ffn(rows):
        x_first, x_second = _unpack_bf16_pairs(xs_ref[rows, :])
        gu = (jnp.dot(x_first, wgu_scr[0:PACKED_WIDTH, :], preferred_element_type=F32)
              + jnp.dot(x_second, wgu_scr[PACKED_WIDTH:, :], preferred_element_type=F32) + bgu_ref[0, 0])
        gate = jnp.minimum(gu[:, :D_FF], SWIGLU_LIMIT)
        up = jnp.clip(gu[:, D_FF:], -SWIGLU_LIMIT, SWIGLU_LIMIT)
        act = (up + 1) * (gate * jax.nn.sigmoid(SWIGLU_ALPHA * gate))
        ys_ref[rows, :] = jnp.dot(act.astype(BF16), wd_scr[...], preferred_element_type=F32) + bd_ref[0, 0]

    n_rows = blk_rows_ref[b]
    half = MOE_BLOCK // 2

    @pl.when(n_rows > half)
    def _():
        ffn(slice(None))

    @pl.when((n_rows > 0) & (n_rows <= half))
    def _():
        ffn(slice(0, half))
        ys_ref[half:, :] = jnp.zeros((MOE_BLOCK - half, D_MODEL), F32)

    @pl.when(n_rows == 0)
    def _():
        ys_ref[...] = jnp.zeros_like(ys_ref)


def _experts(blk_e, next_e, blk_rows, nused, xs, w_gate_up, b_gate_up, w_down, b_down, layer):
    n_blocks = blk_e.shape[0]
    depth = w_gate_up.shape[0]
    grid_spec = pltpu.PrefetchScalarGridSpec(
        num_scalar_prefetch=4,
        grid=(n_blocks,),
        in_specs=[
            pl.BlockSpec((MOE_BLOCK, PACKED_WIDTH), lambda b, be, ne, br, nu: (jnp.minimum(b, nu[0] - 1), 0)),
            pl.BlockSpec((1, 1, 1, 2 * D_FF), lambda b, be, ne, br, nu: (layer, be[b], 0, 0)),
            pl.BlockSpec((1, 1, 1, D_MODEL), lambda b, be, ne, br, nu: (layer, be[b], 0, 0)),
            pl.BlockSpec(memory_space=pl.ANY),
            pl.BlockSpec(memory_space=pl.ANY),
        ],
        out_specs=pl.BlockSpec((MOE_BLOCK, D_MODEL), lambda b, be, ne, br, nu: (b, 0)),
        scratch_shapes=[pltpu.VMEM((2, D_MODEL, 2 * D_FF), F32), pltpu.VMEM((2, D_FF, D_MODEL), F32),
                        pltpu.VMEM((D_MODEL, 2 * D_FF), BF16), pltpu.VMEM((D_FF, D_MODEL), BF16),
                        pltpu.SMEM((1,), I32), pltpu.SemaphoreType.DMA((2, 2))],
    )
    return pl.pallas_call(
        functools.partial(_expert_kernel, layer=layer),
        out_shape=jax.ShapeDtypeStruct((xs.shape[0], D_MODEL), F32),
        grid_spec=grid_spec,
        compiler_params=_cparams("arbitrary"),
        name="moe_experts",
    )(blk_e, next_e, blk_rows, nused, xs, b_gate_up.reshape(depth, N_EXPERTS, 1, 2 * D_FF),
      b_down.reshape(depth, N_EXPERTS, 1, D_MODEL), w_gate_up, w_down)


def _combine_kernel(row_ref, next_row_ref, gate_ref, x_ref, mgate_ref, gfin_ref, ys_hbm, *out_and_scratch,
                    n_ctx, final_norm):
    *out_refs, rows_scr, sems = out_and_scratch
    i = pl.program_id(0)
    n_steps = pl.num_programs(0)
    crow = _cond_row(i, TOKEN_TILE, n_ctx)
    slot = i % 2

    def row_copy(buf, src_row, k, t0, u):
        return pltpu.make_async_copy(ys_hbm.at[pl.ds(src_row, 1)],
                                     rows_scr.at[buf, k, pl.ds(t0, DMA_UNROLL)].at[pl.ds(u, 1)],
                                     sems.at[buf])

    def gather(rows, buf):
        def issue(tt, carry):
            t0 = pl.multiple_of(tt * DMA_UNROLL, DMA_UNROLL)
            for u in range(DMA_UNROLL):
                for k in range(TOP_K):
                    row_copy(buf, rows[0, 0, (t0 + u) * TOP_K + k], k, t0, u).start(priority=k % 2)
            return carry

        lax.fori_loop(0, TOKEN_TILE // DMA_UNROLL, issue, 0)

    @pl.when(i == 0)
    def _():
        gather(row_ref, 0)

    @pl.when(i + 1 < n_steps)
    def _():
        gather(next_row_ref, 1 - slot)

    def drain(t, carry):
        for k in range(TOP_K):
            row_copy(slot, 0, 0, 0, 0).wait()
        return carry

    lax.fori_loop(0, TOKEN_TILE, drain, 0, unroll=DMA_UNROLL)

    def consume(o_ref):
        g = gate_ref[...]
        y = g[:, 0:1] * rows_scr[slot, 0]
        for k in range(1, TOP_K):
            y = y + g[:, k:k + 1] * rows_scr[slot, k]
        x = x_ref[...] + mgate_ref[0, 0, pl.ds(crow, 1), :] * y
        if final_norm:
            x = x * lax.rsqrt(jnp.mean(x * x, axis=-1, keepdims=True) + EPS) * gfin_ref[...]
        o_ref[...] = x

    if len(out_refs) == 1:
        consume(out_refs[0])
    else:
        ctx_ref, lat_ref = out_refs
        is_ctx = i * TOKEN_TILE < n_ctx
        pl.when(is_ctx)(lambda: consume(ctx_ref))
        pl.when(jnp.logical_not(is_ctx))(lambda: consume(lat_ref))


def _combine(rows, gates, x, mod, layer, ys, g_final, n_ctx, final_norm):
    n_tok = x.shape[0]
    nt = n_tok // TOKEN_TILE
    ctx_tiles = n_ctx // TOKEN_TILE
    row_block = (1, 1, TOKEN_TILE * TOP_K)
    rows = rows.reshape(nt, 1, TOKEN_TILE * TOP_K)
    tile = (TOKEN_TILE, D_MODEL)
    if final_norm:
        out_shape = (jax.ShapeDtypeStruct((n_ctx, D_MODEL), F32), jax.ShapeDtypeStruct((n_tok - n_ctx, D_MODEL), F32))
        out_specs = (pl.BlockSpec(tile, lambda i: (jnp.minimum(i, ctx_tiles - 1), 0)),
                     pl.BlockSpec(tile, lambda i: (jnp.maximum(i - ctx_tiles, 0), 0)))
    else:
        out_shape = jax.ShapeDtypeStruct((n_tok, D_MODEL), F32)
        out_specs = pl.BlockSpec(tile, lambda i: (i, 0))
    return pl.pallas_call(
        functools.partial(_combine_kernel, n_ctx=n_ctx, final_norm=final_norm),
        out_shape=out_shape,
        grid=(nt,),
        in_specs=[
            pl.BlockSpec(row_block, lambda i: (i, 0, 0), memory_space=pltpu.SMEM),
            pl.BlockSpec(row_block, lambda i: (jnp.minimum(i + 1, nt - 1), 0, 0), memory_space=pltpu.SMEM),
            pl.BlockSpec((TOKEN_TILE, LANES), lambda i: (i, 0)),
            pl.BlockSpec(tile, lambda i: (i, 0)),
            pl.BlockSpec((1, 1, COND_ROWS, D_MODEL), lambda i: (layer, 5, 0, 0)),
            pl.BlockSpec((1, D_MODEL), lambda i: (0, 0)),
            pl.BlockSpec(memory_space=pl.ANY),
        ],
        out_specs=out_specs,
        scratch_shapes=[pltpu.VMEM((2, TOP_K, TOKEN_TILE, D_MODEL), F32), pltpu.SemaphoreType.DMA((2,))],
        compiler_params=_cparams("arbitrary"),
        name="moe_combine",
    )(rows, rows, gates, x, mod, g_final, ys)


def _rope_tables():
    t = np.arange(LAT_SEQ)
    row = (t // GRID_W).astype(np.float32)
    col = (t % GRID_W).astype(np.float32)
    half = HEAD_DIM // 2
    inv = jnp.asarray(ROPE_THETA, F32) ** (-jnp.arange(0, half, 2, dtype=F32) / half)
    ang = jnp.concatenate([row[:, None] * inv, col[:, None] * inv], axis=-1)
    cos, sin = jnp.cos(ang), jnp.sin(ang)
    return jnp.concatenate([cos, cos], axis=-1), jnp.concatenate([-sin, sin], axis=-1)


def _slot_rows(codes, starts):
    expert = lax.shift_right_logical(codes, SLOT_SHIFT)
    base = jnp.sum(jnp.where(expert[..., None] == jnp.arange(N_EXPERTS, dtype=I32), starts, 0), axis=-1)
    return base + (codes & (SLOT_BASE - 1))


def _routing_plan(counts_f, n_blocks):
    counts = counts_f[0, :N_EXPERTS].astype(I32)
    padded = (counts + MOE_BLOCK - 1) // MOE_BLOCK * MOE_BLOCK
    pend = jnp.cumsum(padded)
    starts = pend - padded
    nused = pend[-1] // MOE_BLOCK
    blk = jnp.minimum(jnp.arange(n_blocks, dtype=I32), nused - 1)
    blk_e = jnp.sum((pend[None, :] <= (blk * MOE_BLOCK)[:, None]).astype(I32), axis=1)
    blk_e = jnp.minimum(blk_e, N_EXPERTS - 1)
    ids = jnp.arange(N_EXPERTS, dtype=I32)
    later = (ids[None, :] > blk_e[:, None]) & (counts > 0)[None, :]
    next_e = jnp.min(jnp.where(later, ids[None, :], N_EXPERTS), axis=1)
    next_e = jnp.where(next_e == N_EXPERTS, blk_e, next_e).astype(I32)
    mine = ids[None, :] == blk_e[:, None]
    group_end = jnp.sum(jnp.where(mine, (starts + counts)[None, :], 0), axis=1)
    first_row = jnp.arange(n_blocks, dtype=I32) * MOE_BLOCK
    blk_rows = jnp.where(jnp.arange(n_blocks) < nused, jnp.clip(group_end - first_row, 0, MOE_BLOCK), 0).astype(I32)
    return counts, starts.astype(I32), nused.reshape(1).astype(I32), blk_e, next_e, blk_rows


def _forward(x_prompt, x_sample, cache_na_k, cache_na_v, cache_gqa_k, cache_gqa_v, c, c_ctx,
             w_ada, b_ada, g_norm1, g_norm2, w_in, w_pool, pool_scale, na_rel_bias, g_q, g_k,
             w_out, w_router, b_router, w_gate_up, b_gate_up, w_down, b_down, g_final):
    depth = w_in.shape[0]
    nb_ctx, nb_lat = x_prompt.shape[0], x_sample.shape[0]
    n_ctx, n_lat = nb_ctx * CTX_SEQ, nb_lat * LAT_SEQ
    n_tok = n_ctx + n_lat
    assert x_prompt.shape[1:] == (CTX_SEQ, D_MODEL) and x_sample.shape[1:] == (LAT_SEQ, D_MODEL)
    assert n_ctx % LAT_SEQ == 0 and 1 + nb_lat <= COND_ROWS
    n_blocks = n_tok * TOP_K // MOE_BLOCK + N_EXPERTS

    x = jnp.concatenate([x_prompt.reshape(n_ctx, D_MODEL), x_sample.reshape(n_lat, D_MODEL)], axis=0)
    cond = jnp.zeros((COND_ROWS, D_MODEL), F32).at[0].set(c_ctx).at[1:1 + nb_lat].set(c)
    mod = _ada_mod(cond, w_ada, b_ada)

    cos2, sin2 = _rope_tables()
    na_bias = _na_bias_tables(na_rel_bias)
    w_in_b = w_in.astype(BF16)
    w_out_b = w_out.astype(BF16)
    w_pool_b = w_pool.astype(BF16)
    w_router_b = jnp.pad(w_router, ((0, 0), (0, 0), (0, LANES - N_EXPERTS))).astype(BF16)
    b_router_p = jnp.pad(b_router, ((0, 0), (0, LANES - N_EXPERTS)), constant_values=NEG)
    cna_k = cache_na_k.reshape(nb_lat, depth, CTX_SEQ, NA_WIDTH).astype(BF16)
    cna_v = cache_na_v.reshape(nb_lat, depth, CTX_SEQ, NA_WIDTH).astype(BF16)
    cgq_k = cache_gqa_k.reshape(nb_lat, depth, CTX_SEQ, GQA_KV_WIDTH).astype(BF16)
    cgq_v = cache_gqa_v.reshape(nb_lat, depth, CTX_SEQ, GQA_KV_WIDTH).astype(BF16)

    new_kv = []
    for l in range(depth):
        u, kv_f, proj_b = _inproj(x, mod, l, g_norm1[l][None], w_in_b, g_q[l][None], g_k[l][None],
                                  cos2, sin2, n_ctx)
        new_kv.append(kv_f[:n_ctx])
        ya = _pool(u, w_pool_b[l], pool_scale[l][None], n_ctx)
        yb = _na_attention(proj_b, cna_k, cna_v, na_bias, l, n_ctx)
        yc = _gqa_attention(proj_b, cgq_k, cgq_v, l, n_ctx)
        x, h, logits = _outproj(x, ya, yb, yc, w_out_b, mod, l, g_norm2[l][None],
                                w_router_b[l], b_router_p[l][None], n_ctx)
        codes, gates, counts_f = _route(logits)
        counts, starts, nused, blk_e, next_e, blk_rows = _routing_plan(counts_f, n_blocks)
        rows = _slot_rows(codes[:, :TOP_K], starts)
        xs = _dispatch(counts, starts, nused, rows, h, n_blocks)
        ys = _experts(blk_e, next_e, blk_rows, nused, xs, w_gate_up, b_gate_up, w_down, b_down, l)
        x = _combine(rows, gates, x, mod, l, ys, g_final[None], n_ctx, final_norm=(l == depth - 1))

    y_ctx, y_lat = x
    y_prompt = y_ctx.reshape(nb_ctx, CTX_SEQ, D_MODEL)
    y_sample = y_lat.reshape(nb_lat, LAT_SEQ, D_MODEL)

    def stack(col, width):
        per_layer = [p[:, col:col + width].reshape(nb_ctx, CTX_SEQ, width // HEAD_DIM, HEAD_DIM) for p in new_kv]
        return jnp.stack(per_layer, axis=1)

    return (y_prompt, y_sample, stack(KVF_KN, NA_WIDTH), stack(KVF_VN, NA_WIDTH),
            stack(KVF_KG, GQA_KV_WIDTH), stack(KVF_VG, GQA_KV_WIDTH))


def kernel(x_prompt, x_sample, cache_na_k, cache_na_v, cache_gqa_k, cache_gqa_v, c, c_ctx, w_ada, b_ada,
           g_norm1, g_norm2, w_in, w_pool, pool_scale, na_rel_bias, g_q, g_k, w_out, w_router, b_router,
           w_gate_up, b_gate_up, w_down, b_down, g_final):
    return _forward(x_prompt, x_sample, cache_na_k, cache_na_v, cache_gqa_k, cache_gqa_v, c, c_ctx,
                    w_ada, b_ada, g_norm1, g_norm2, w_in, w_pool, pool_scale, na_rel_bias, g_q, g_k,
                    w_out, w_router, b_router, w_gate_up, b_gate_up, w_down, b_down, g_final)
```

```python
import functools
import math

import numpy as np
import jax
import jax.numpy as jnp
from jax import lax
from jax.experimental import pallas as pl
from jax.experimental.pallas import tpu as pltpu

F32 = jnp.float32
BF16 = jnp.bfloat16
I32 = jnp.int32

D_MODEL = 2048
HEAD_DIM = 128
CTX_SEQ = 256
LAT_SEQ = 2048
GRID_W = 64
GRID_ROWS = LAT_SEQ // GRID_W
POOL_WINDOWS = (2, 4, 8, 16)
POOL_GROUP = 128
POOL_WIDTH = 512
NA_HEADS = 4
NA_WIDTH = 512
NA_ROWS = 8
NA_COLS = 16
GQA_HEADS = 8
GQA_KV_HEADS = 2
GQA_GROUP = GQA_HEADS // GQA_KV_HEADS
GQA_WIDTH = 1024
GQA_KV_WIDTH = 256
IN_WIDTH = 3584
N_EXPERTS = 32
TOP_K = 4
D_FF = 512
SWIGLU_LIMIT = 7.0
SWIGLU_ALPHA = 1.702
ROPE_THETA = 10000.0
EPS = 1e-6
NEG = -1e30

COL_U, COL_QN, COL_KN, COL_VN, COL_QG, COL_KG, COL_VG = 0, 512, 1024, 1536, 2048, 3072, 3328
ACT_QG, ACT_QN, ACT_KN, ACT_VN, ACT_KG, ACT_VG, ACT_WIDTH = 0, 1024, 1536, 2048, 2560, 2816, 3072
KVF_KN, KVF_VN, KVF_KG, KVF_VG, KVF_WIDTH = 0, 512, 1024, 1280, 1536

LANES = 128
COND_ROWS = 16
ATT_TILE = 256
NA_TILE_ROWS = ATT_TILE // GRID_W
NA_WIN_ROWS = NA_ROWS + NA_TILE_ROWS - 1
NA_WIN = NA_WIN_ROWS * GRID_W
INPROJ_TM = 512
OUTPROJ_TM = 512
ROUTE_TM = 512
MOE_BLOCK = 512
TOKEN_TILE = 256
DISPATCH_TILE = 512
SLOT_SHIFT = 16
SLOT_BASE = 1 << SLOT_SHIFT
DMA_UNROLL = 8
VMEM_LIMIT = 56 * 1024 * 1024


def _cparams(*sem):
    return pltpu.CompilerParams(dimension_semantics=sem, vmem_limit_bytes=VMEM_LIMIT)


def _cond_row(tile, tile_rows, n_ctx):
    row0 = tile * tile_rows
    return jnp.where(row0 < n_ctx, 0, 1 + (row0 - n_ctx) // LAT_SEQ)


def _head(h):
    return slice(h * HEAD_DIM, (h + 1) * HEAD_DIM)


def _stream_specs(stream, tm, tile_of_step, n_ctx):
    if len(stream) == 1:
        return [pl.BlockSpec((tm, D_MODEL), lambda i: (tile_of_step(i), 0))]
    ctx_tiles = n_ctx // tm
    return [pl.BlockSpec((tm, D_MODEL), lambda i: (jnp.minimum(tile_of_step(i), ctx_tiles - 1), 0)),
            pl.BlockSpec((tm, D_MODEL), lambda i: (jnp.maximum(tile_of_step(i) - ctx_tiles, 0), 0))]


def _stream_rows(refs, tile, tm, n_ctx, rows):
    if len(refs) == 1:
        return refs[0][rows, :]
    return jnp.where(tile * tm < n_ctx, refs[0][rows, :], refs[1][rows, :])


U32 = jnp.uint32
PACKED_WIDTH = D_MODEL // 2


def _pack_bf16_pairs(x):
    half = x.shape[1] // 2
    hi = pltpu.bitcast(x[:, :half].astype(BF16).astype(F32), U32)
    lo = pltpu.bitcast(x[:, half:].astype(BF16).astype(F32), U32)
    return hi | (lo >> 16)


def _unpack_bf16_pairs(p):
    first = pltpu.bitcast(p & jnp.uint32(0xFFFF0000), F32).astype(BF16)
    second = pltpu.bitcast(p << 16, F32).astype(BF16)
    return first, second


def _ada_kernel(c_ref, w_ref, b_ref, o_ref):
    c = c_ref[...]
    s = (c * jax.nn.sigmoid(c)).astype(BF16)
    o_ref[0, 0] = jnp.dot(s, w_ref[0].astype(BF16), preferred_element_type=F32) + b_ref[0]


def _ada_mod(cond, w_ada, b_ada):
    depth = w_ada.shape[0]
    tn = 1024
    per = D_MODEL // tn
    return pl.pallas_call(
        _ada_kernel,
        out_shape=jax.ShapeDtypeStruct((depth, 6, COND_ROWS, D_MODEL), F32),
        grid=(depth, 6 * per),
        in_specs=[
            pl.BlockSpec((COND_ROWS, D_MODEL), lambda l, j: (0, 0)),
            pl.BlockSpec((1, D_MODEL, tn), lambda l, j: (l, 0, j)),
            pl.BlockSpec((1, 1, tn), lambda l, j: (l, 0, j)),
        ],
        out_specs=pl.BlockSpec((1, 1, COND_ROWS, tn), lambda l, j: (l, j // per, 0, j % per)),
        compiler_params=_cparams("arbitrary", "arbitrary"),
        name="ada_mod",
    )(cond, w_ada, b_ada.reshape(depth, 1, 6 * D_MODEL))


def _head_rms(x, g):
    return x * lax.rsqrt(jnp.mean(x * x, axis=-1, keepdims=True) + EPS) * g


NORM_CHUNKS = 8


def _inproj_kernel(*refs, n_ctx, n_tiles, n_stream):
    xn_refs, refs = refs[:n_stream], refs[n_stream:]
    (shift_ref, scale_ref, g1_ref, w_ref, gq_ref, gk_ref, cos_ref, sin_ref,
     u_ref, kv_ref, act_ref, h_even, h_odd) = refs
    i = pl.program_id(0)
    nxt = jnp.minimum(i, n_tiles - 1)
    is_lat = jnp.maximum(i - 1, 0) * INPROJ_TM >= n_ctx
    cos = jnp.where(is_lat, cos_ref[...], 1.0)
    sin = jnp.where(is_lat, sin_ref[...], 0.0)

    def normalize(x_refs, tile, h_ref, rows):
        crow = _cond_row(tile, INPROJ_TM, n_ctx)
        x = _stream_rows(x_refs, tile, INPROJ_TM, n_ctx, rows)
        y = x * lax.rsqrt(jnp.mean(x * x, axis=-1, keepdims=True) + EPS) * g1_ref[...]
        h_ref[rows, :] = (y * (1 + scale_ref[0, 0, pl.ds(crow, 1), :])
                          + shift_ref[0, 0, pl.ds(crow, 1), :]).astype(BF16)

    def normed_rope(xh, g_ref):
        xh = _head_rms(xh, g_ref[...])
        return xh * cos + pltpu.roll(xh, HEAD_DIM // 2, 1) * sin

    def project(h_ref, h_next_ref):
        chunk = INPROJ_TM // NORM_CHUNKS
        steps = iter(range(NORM_CHUNKS))

        def proj(col, width):
            acc = jnp.dot(h_ref[...], w_ref[0, :, col:col + width], preferred_element_type=F32)
            c = next(steps)
            normalize(xn_refs, nxt, h_next_ref, slice(c * chunk, (c + 1) * chunk))
            return acc

        for half in range(GQA_KV_HEADS):
            q = proj(COL_QG + half * GQA_GROUP * HEAD_DIM, GQA_GROUP * HEAD_DIM)
            for g in range(GQA_GROUP):
                col = ACT_QG + (half * GQA_GROUP + g) * HEAD_DIM
                act_ref[:, col:col + HEAD_DIM] = normed_rope(q[:, _head(g)], gq_ref).astype(BF16)
        kg = proj(COL_KG, GQA_KV_WIDTH)
        for kh in range(GQA_KV_HEADS):
            v = normed_rope(kg[:, _head(kh)], gk_ref)
            kv_ref[:, KVF_KG + kh * HEAD_DIM:KVF_KG + (kh + 1) * HEAD_DIM] = v
            act_ref[:, ACT_KG + kh * HEAD_DIM:ACT_KG + (kh + 1) * HEAD_DIM] = v.astype(BF16)
        vg = proj(COL_VG, GQA_KV_WIDTH)
        kv_ref[:, KVF_VG:KVF_VG + GQA_KV_WIDTH] = vg
        act_ref[:, ACT_VG:ACT_VG + GQA_KV_WIDTH] = vg.astype(BF16)
        for col, kv_col, act_col in ((COL_KN, KVF_KN, ACT_KN), (COL_VN, KVF_VN, ACT_VN)):
            v = proj(col, NA_WIDTH)
            kv_ref[:, kv_col:kv_col + NA_WIDTH] = v
            act_ref[:, act_col:act_col + NA_WIDTH] = v.astype(BF16)
        act_ref[:, ACT_QN:ACT_QN + NA_WIDTH] = proj(COL_QN, NA_WIDTH).astype(BF16)
        u_ref[...] = proj(COL_U, POOL_WIDTH)

    @pl.when(i == 0)
    def _():
        normalize(xn_refs, 0, h_even, slice(None))

    @pl.when(i % 2 == 1)
    def _():
        project(h_even, h_odd)

    @pl.when((i > 0) & (i % 2 == 0))
    def _():
        project(h_odd, h_even)


def _inproj(stream, mod, layer, g1, w_in_b, g_q, g_k, cos2, sin2, n_ctx):
    n_tok = sum(a.shape[0] for a in stream)
    tm = INPROJ_TM
    lat_tiles = LAT_SEQ // tm
    ctx_tiles = n_ctx // tm
    n_tiles = n_tok // tm

    def rope_map(i):
        return (jnp.maximum(i - 1 - ctx_tiles, 0) % lat_tiles, 0)

    def rows(width):
        return pl.BlockSpec((tm, width), lambda i: (jnp.maximum(i - 1, 0), 0))

    return pl.pallas_call(
        functools.partial(_inproj_kernel, n_ctx=n_ctx, n_tiles=n_tiles, n_stream=len(stream)),
        out_shape=(jax.ShapeDtypeStruct((n_tok, POOL_WIDTH), F32),
                   jax.ShapeDtypeStruct((n_tok, KVF_WIDTH), F32),
                   jax.ShapeDtypeStruct((n_tok, ACT_WIDTH), BF16)),
        grid=(n_tiles + 1,),
        in_specs=[
            *_stream_specs(stream, tm, lambda i: jnp.minimum(i, n_tiles - 1), n_ctx),
            pl.BlockSpec((1, 1, COND_ROWS, D_MODEL), lambda i: (layer, 0, 0, 0)),
            pl.BlockSpec((1, 1, COND_ROWS, D_MODEL), lambda i: (layer, 1, 0, 0)),
            pl.BlockSpec((1, D_MODEL), lambda i: (0, 0)),
            pl.BlockSpec((1, D_MODEL, IN_WIDTH), lambda i: (layer, 0, 0), pipeline_mode=pl.Buffered(1)),
            pl.BlockSpec((1, HEAD_DIM), lambda i: (0, 0)),
            pl.BlockSpec((1, HEAD_DIM), lambda i: (0, 0)),
            pl.BlockSpec((tm, HEAD_DIM), rope_map),
            pl.BlockSpec((tm, HEAD_DIM), rope_map),
        ],
        out_specs=(rows(POOL_WIDTH), rows(KVF_WIDTH), rows(ACT_WIDTH)),
        scratch_shapes=[pltpu.VMEM((tm, D_MODEL), BF16), pltpu.VMEM((tm, D_MODEL), BF16)],
        compiler_params=_cparams("arbitrary"),
        name="norm_inproj",
    )(*stream, mod, mod, g1, w_in_b, g_q, g_k, cos2, sin2)


POOL_HALO = 8


def _pool_kernel(up_ref, uc_ref, un_ref, wp_ref, ps_ref, o_ref, pad_scr, *, n_ctx):
    i = pl.program_id(0)
    row0 = i * ATT_TILE
    is_ctx = row0 < n_ctx
    t0 = jnp.where(is_ctx, 0, (row0 - n_ctx) % LAT_SEQ)
    seq = jnp.where(is_ctx, CTX_SEQ, LAT_SEQ)
    first = t0 == 0
    last = t0 + ATT_TILE == seq
    zeros = jnp.zeros((POOL_HALO, POOL_WIDTH), F32)
    pad_scr[0:POOL_HALO, :] = jnp.where(first, zeros, up_ref[ATT_TILE - POOL_HALO:ATT_TILE, :])
    pad_scr[POOL_HALO:POOL_HALO + ATT_TILE, :] = uc_ref[...]
    pad_scr[POOL_HALO + ATT_TILE:, :] = jnp.where(last, zeros, un_ref[0:POOL_HALO, :])
    t = t0 + lax.broadcasted_iota(I32, (ATT_TILE, 1), 0)
    for g, w in enumerate(POOL_WINDOWS):
        cols = slice(g * POOL_GROUP, (g + 1) * POOL_GROUP)
        acc = pad_scr[pl.ds(POOL_HALO - w // 2, ATT_TILE), cols]
        for d in range(-w // 2 + 1, w // 2):
            acc = acc + pad_scr[pl.ds(POOL_HALO + d, ATT_TILE), cols]
        cnt = (jnp.minimum(t + w // 2, seq) - jnp.maximum(t - w // 2, 0)).astype(F32)
        pooled = acc / cnt - uc_ref[:, cols]
        y = jnp.dot(pooled.astype(BF16), wp_ref[g], preferred_element_type=F32)
        o_ref[:, cols] = (y * ps_ref[:, cols]).astype(BF16)


def _pool(proj_f, w_pool_b, pool_scale, n_ctx):
    n_tok = proj_f.shape[0]
    nt = n_tok // ATT_TILE
    return pl.pallas_call(
        functools.partial(_pool_kernel, n_ctx=n_ctx),
        out_shape=jax.ShapeDtypeStruct((n_tok, POOL_WIDTH), BF16),
        grid=(nt,),
        in_specs=[
            pl.BlockSpec((ATT_TILE, POOL_WIDTH), lambda i: (jnp.maximum(i - 1, 0), 0)),
            pl.BlockSpec((ATT_TILE, POOL_WIDTH), lambda i: (i, 0)),
            pl.BlockSpec((ATT_TILE, POOL_WIDTH), lambda i: (jnp.minimum(i + 1, nt - 1), 0)),
            pl.BlockSpec((len(POOL_WINDOWS), POOL_GROUP, POOL_GROUP), lambda i: (0, 0, 0)),
            pl.BlockSpec((1, POOL_WIDTH), lambda i: (0, 0)),
        ],
        out_specs=pl.BlockSpec((ATT_TILE, POOL_WIDTH), lambda i: (i, 0)),
        scratch_shapes=[pltpu.VMEM((ATT_TILE + 2 * POOL_HALO, POOL_WIDTH), F32)],
        compiler_params=_cparams("arbitrary"),
        name="pool_mixer",
    )(proj_f, proj_f, proj_f, w_pool_b, pool_scale)


_NT_DIMS = (((1,), (1,)), ((), ()))
ATT_SCALE = HEAD_DIM ** -0.5
EXP2_SCALE = ATT_SCALE * math.log2(math.e)


def _scores(q, k):
    return lax.dot_general(q, k, _NT_DIMS, preferred_element_type=F32)


def _attend_one(q, k, v):
    s = _scores(q, k)
    p = jnp.exp2((s - jnp.max(s, axis=-1, keepdims=True)) * EXP2_SCALE)
    o = jnp.dot(p.astype(BF16), v, preferred_element_type=F32)
    return o / jnp.sum(p, axis=-1, keepdims=True)


def _attend_two(s1, v1, s2, v2):
    m = jnp.maximum(jnp.max(s1, axis=-1, keepdims=True), jnp.max(s2, axis=-1, keepdims=True))
    p1 = jnp.exp2((s1 - m) * EXP2_SCALE)
    p2 = jnp.exp2((s2 - m) * EXP2_SCALE)
    o = (jnp.dot(p1.astype(BF16), v1, preferred_element_type=F32)
         + jnp.dot(p2.astype(BF16), v2, preferred_element_type=F32))
    return o / (jnp.sum(p1, axis=-1, keepdims=True) + jnp.sum(p2, axis=-1, keepdims=True))


def _na_kernel(q_ref, kc_ref, vc_ref, kl_ref, vl_ref, ck_ref, cv_ref, bias_ref, o_ref, *, n_ctx_steps):
    i = pl.program_id(0)

    @pl.when(i < n_ctx_steps)
    def _():
        for h in range(NA_HEADS):
            o = _attend_one(q_ref[:, _head(h)], kc_ref[:, _head(h)], vc_ref[:, _head(h)])
            o_ref[:, _head(h)] = o.astype(BF16)

    @pl.when(i >= n_ctx_steps)
    def _():
        tile = (i - n_ctx_steps) % (LAT_SEQ // ATT_TILE)
        win_row = jnp.clip(tile * NA_TILE_ROWS - NA_ROWS // 2, 0, GRID_ROWS - NA_WIN_ROWS)
        key0 = pl.multiple_of(win_row * GRID_W, GRID_W)
        def scores(h):
            q = q_ref[:, _head(h)]
            return (_scores(q, kl_ref[pl.ds(key0, NA_WIN), _head(h)]) + bias_ref[0, 0, h],
                    _scores(q, ck_ref[0, 0, :, _head(h)]))

        nxt = scores(0)
        for h in range(NA_HEADS):
            s_lat, s_ctx = nxt
            if h + 1 < NA_HEADS:
                nxt = scores(h + 1)
            o = _attend_two(s_lat, vl_ref[pl.ds(key0, NA_WIN), _head(h)],
                            s_ctx, cv_ref[0, 0, :, _head(h)])
            o_ref[:, _head(h)] = o.astype(BF16)


def _na_bias_tables(rel_bias):
    depth = rel_bias.shape[0]
    half = NA_ROWS // 2
    r0 = np.array([0, half, GRID_ROWS - NA_TILE_ROWS])[:, None, None]
    win = np.clip(r0 - half, 0, GRID_ROWS - NA_WIN_ROWS)
    qr = r0 + np.arange(NA_TILE_ROWS)[None, :, None]
    kr = win + np.arange(NA_WIN_ROWS)[None, None, :]
    rs = np.clip(qr - half, 0, GRID_ROWS - NA_ROWS)
    row_ok = (kr >= rs) & (kr < rs + NA_ROWS)
    dr = np.clip(kr - qr + NA_ROWS - 1, 0, 2 * NA_ROWS - 2)
    qc = np.arange(GRID_W)[:, None]
    kc = np.arange(GRID_W)[None, :]
    cs = np.clip(qc - NA_COLS // 2, 0, GRID_W - NA_COLS)
    col_ok = (kc >= cs) & (kc < cs + NA_COLS)
    dc = np.clip(kc - qc + NA_COLS - 1, 0, 2 * NA_COLS - 2)
    sel_r = (dr[..., None] == np.arange(2 * NA_ROWS - 1)).astype(np.float32)
    sel_c = (dc[..., None] == np.arange(2 * NA_COLS - 1)).astype(np.float32)
    vals = jnp.einsum("sqka,lhab,xyb->lshqxky", sel_r, rel_bias, sel_c, precision=lax.Precision.HIGHEST)
    ok = row_ok[:, :, None, :, None] & col_ok[None, None, :, None, :]
    table = jnp.where(ok[None, :, None], vals / ATT_SCALE, NEG)
    return table.reshape(depth, 3, NA_HEADS, ATT_TILE, NA_WIN).astype(F32)


def _na_attention(proj_b, cache_k, cache_v, bias, layer, n_ctx):
    n_tok = proj_b.shape[0]
    n_ctx_steps = n_ctx // ATT_TILE
    lat_tiles = LAT_SEQ // ATT_TILE
    lat_blk0 = n_ctx // LAT_SEQ
    wq = NA_WIDTH

    def ctx_blk(i):
        return jnp.minimum(i, n_ctx_steps - 1)

    def req(i):
        return jnp.maximum(i - n_ctx_steps, 0) // lat_tiles

    def kind(i):
        tile = jnp.maximum(i - n_ctx_steps, 0) % lat_tiles
        return jnp.where(tile == 0, 0, jnp.where(tile == lat_tiles - 1, 2, 1))

    return pl.pallas_call(
        functools.partial(_na_kernel, n_ctx_steps=n_ctx_steps),
        out_shape=jax.ShapeDtypeStruct((n_tok, NA_WIDTH), BF16),
        grid=(n_tok // ATT_TILE,),
        in_specs=[
            pl.BlockSpec((ATT_TILE, wq), lambda i: (i, ACT_QN // wq)),
            pl.BlockSpec((ATT_TILE, wq), lambda i: (ctx_blk(i), ACT_KN // wq)),
            pl.BlockSpec((ATT_TILE, wq), lambda i: (ctx_blk(i), ACT_VN // wq)),
            pl.BlockSpec((LAT_SEQ, wq), lambda i: (lat_blk0 + req(i), ACT_KN // wq)),
            pl.BlockSpec((LAT_SEQ, wq), lambda i: (lat_blk0 + req(i), ACT_VN // wq)),
            pl.BlockSpec((1, 1, CTX_SEQ, wq), lambda i: (req(i), layer, 0, 0)),
            pl.BlockSpec((1, 1, CTX_SEQ, wq), lambda i: (req(i), layer, 0, 0)),
            pl.BlockSpec((1, 1, NA_HEADS, ATT_TILE, NA_WIN), lambda i: (layer, kind(i), 0, 0, 0)),
        ],
        out_specs=pl.BlockSpec((ATT_TILE, wq), lambda i: (i, 0)),
        compiler_params=_cparams("arbitrary"),
        name="na_attention",
    )(proj_b, proj_b, proj_b, proj_b, proj_b, cache_k, cache_v, bias)


def _gqa_kernel(q_ref, kc_ref, vc_ref, kl_ref, vl_ref, ck_ref, cv_ref, o_ref, *, n_ctx_steps):
    i = pl.program_id(0)

    def kv_head(h):
        return _head(h // GQA_GROUP)

    @pl.when(i < n_ctx_steps)
    def _():
        for h in range(GQA_HEADS):
            o = _attend_one(q_ref[:, _head(h)], kc_ref[:, kv_head(h)], vc_ref[:, kv_head(h)])
            o_ref[:, _head(h)] = o.astype(BF16)

    @pl.when(i >= n_ctx_steps)
    def _():
        def scores(h):
            q = q_ref[:, _head(h)]
            return _scores(q, kl_ref[:, kv_head(h)]), _scores(q, ck_ref[0, 0, :, kv_head(h)])

        nxt = scores(0)
        for h in range(GQA_HEADS):
            s_lat, s_ctx = nxt
            if h + 1 < GQA_HEADS:
                nxt = scores(h + 1)
            o = _attend_two(s_lat, vl_ref[:, kv_head(h)], s_ctx, cv_ref[0, 0, :, kv_head(h)])
            o_ref[:, _head(h)] = o.astype(BF16)


def _gqa_attention(proj_b, cache_k, cache_v, layer, n_ctx):
    n_tok = proj_b.shape[0]
    n_ctx_steps = n_ctx // ATT_TILE
    lat_tiles = LAT_SEQ // ATT_TILE
    lat_blk0 = n_ctx // LAT_SEQ
    wkv = GQA_KV_WIDTH

    def ctx_blk(i):
        return jnp.minimum(i, n_ctx_steps - 1)

    def req(i):
        return jnp.maximum(i - n_ctx_steps, 0) // lat_tiles

    return pl.pallas_call(
        functools.partial(_gqa_kernel, n_ctx_steps=n_ctx_steps),
        out_shape=jax.ShapeDtypeStruct((n_tok, GQA_WIDTH), BF16),
        grid=(n_tok // ATT_TILE,),
        in_specs=[
            pl.BlockSpec((ATT_TILE, GQA_WIDTH), lambda i: (i, ACT_QG // GQA_WIDTH)),
            pl.BlockSpec((ATT_TILE, wkv), lambda i: (ctx_blk(i), ACT_KG // wkv)),
            pl.BlockSpec((ATT_TILE, wkv), lambda i: (ctx_blk(i), ACT_VG // wkv)),
            pl.BlockSpec((LAT_SEQ, wkv), lambda i: (lat_blk0 + req(i), ACT_KG // wkv)),
            pl.BlockSpec((LAT_SEQ, wkv), lambda i: (lat_blk0 + req(i), ACT_VG // wkv)),
            pl.BlockSpec((1, 1, CTX_SEQ, wkv), lambda i: (req(i), layer, 0, 0)),
            pl.BlockSpec((1, 1, CTX_SEQ, wkv), lambda i: (req(i), layer, 0, 0)),
        ],
        out_specs=pl.BlockSpec((ATT_TILE, GQA_WIDTH), lambda i: (i, 0)),
        compiler_params=_cparams("arbitrary"),
        name="gqa_attention",
    )(proj_b, proj_b, proj_b, proj_b, proj_b, cache_k, cache_v)


OUTPROJ_CHUNKS = 4


def _outproj_kernel(*refs, n_ctx, n_tiles, n_stream):
    x_refs, refs = refs[:n_stream], refs[n_stream:]
    (ya_ref, yb_ref, yc_ref, wo_ref, gate_ref, shift_ref, scale_ref, g2_ref, wr_ref, br_ref,
     xo_ref, h_ref, lg_ref, mix_even, mix_odd) = refs
    i = pl.program_id(0)
    prev_tile = jnp.maximum(i - 1, 0)
    crow = _cond_row(prev_tile, OUTPROJ_TM, n_ctx)
    cols = D_MODEL // OUTPROJ_CHUNKS
    rows = OUTPROJ_TM // OUTPROJ_CHUNKS

    def product(mix_ref, c):
        cs = slice(c * cols, (c + 1) * cols)
        mix_ref[:, cs] = (
            jnp.dot(ya_ref[...], wo_ref[0, 0:POOL_WIDTH, cs], preferred_element_type=F32)
            + jnp.dot(yb_ref[...], wo_ref[0, POOL_WIDTH:POOL_WIDTH + NA_WIDTH, cs], preferred_element_type=F32)
            + jnp.dot(yc_ref[...], wo_ref[0, POOL_WIDTH + NA_WIDTH:, cs], preferred_element_type=F32))

    def finish(mix_ref, c):
        rs = slice(c * rows, (c + 1) * rows)
        x = (_stream_rows(x_refs, prev_tile, OUTPROJ_TM, n_ctx, rs)
             + gate_ref[0, 0, pl.ds(crow, 1), :] * mix_ref[rs, :])
        xo_ref[rs, :] = x
        y = x * lax.rsqrt(jnp.mean(x * x, axis=-1, keepdims=True) + EPS) * g2_ref[...]
        h = y * (1 + scale_ref[0, 0, pl.ds(crow, 1), :]) + shift_ref[0, 0, pl.ds(crow, 1), :]
        h_ref[rs, :] = _pack_bf16_pairs(h)
        lg_ref[rs, :] = jnp.dot(h.astype(BF16), wr_ref[...], preferred_element_type=F32) + br_ref[...]

    @pl.when(i == 0)
    def _():
        for c in range(OUTPROJ_CHUNKS):
            product(mix_even, c)

    for parity, (cur, prev) in enumerate(((mix_even, mix_odd), (mix_odd, mix_even))):
        @pl.when((i > 0) & (i < n_tiles) & (i % 2 == parity))
        def _(cur=cur, prev=prev):
            for c in range(OUTPROJ_CHUNKS):
                product(cur, c)
                finish(prev, c)

    @pl.when(i == n_tiles)
    def _():
        last = mix_even if (n_tiles - 1) % 2 == 0 else mix_odd
        for c in range(OUTPROJ_CHUNKS):
            finish(last, c)


def _outproj(stream, ya, yb, yc, w_out_b, mod, layer, g2, w_router_b, b_router_p, n_ctx):
    n_tok = ya.shape[0]
    tm = OUTPROJ_TM
    n_tiles = n_tok // tm

    def mod_spec(chunk):
        return pl.BlockSpec((1, 1, COND_ROWS, D_MODEL), lambda i: (layer, chunk, 0, 0))

    def cur(width):
        return pl.BlockSpec((tm, width), lambda i: (jnp.minimum(i, n_tiles - 1), 0))

    def prev(width):
        return pl.BlockSpec((tm, width), lambda i: (jnp.maximum(i - 1, 0), 0))

    return pl.pallas_call(
        functools.partial(_outproj_kernel, n_ctx=n_ctx, n_tiles=n_tiles, n_stream=len(stream)),
        out_shape=(jax.ShapeDtypeStruct((n_tok, D_MODEL), F32),
                   jax.ShapeDtypeStruct((n_tok, PACKED_WIDTH), U32),
                   jax.ShapeDtypeStruct((n_tok, LANES), F32)),
        grid=(n_tiles + 1,),
        in_specs=[
            *_stream_specs(stream, tm, lambda i: jnp.maximum(i - 1, 0), n_ctx),
            cur(POOL_WIDTH), cur(NA_WIDTH), cur(GQA_WIDTH),
            pl.BlockSpec((1, D_MODEL, D_MODEL), lambda i: (layer, 0, 0), pipeline_mode=pl.Buffered(1)),
            mod_spec(2), mod_spec(3), mod_spec(4),
            pl.BlockSpec((1, D_MODEL), lambda i: (0, 0)),
            pl.BlockSpec((D_MODEL, LANES), lambda i: (0, 0)),
            pl.BlockSpec((1, LANES), lambda i: (0, 0)),
        ],
        out_specs=(prev(D_MODEL), prev(PACKED_WIDTH), prev(LANES)),
        scratch_shapes=[pltpu.VMEM((tm, D_MODEL), F32), pltpu.VMEM((tm, D_MODEL), F32)],
        compiler_params=_cparams("arbitrary"),
        name="outproj_norm_router",
    )(*stream, ya, yb, yc, w_out_b, mod, mod, mod, g2, w_router_b, b_router_p)


def _route_kernel(lg_ref, code_ref, gate_ref, cnt_ref, carry_scr):
    i = pl.program_id(0)

    @pl.when(i == 0)
    def _():
        carry_scr[...] = jnp.zeros_like(carry_scr)

    tm = lg_ref.shape[0]
    l = lg_ref[...]
    lane = lax.broadcasted_iota(I32, (tm, LANES), 1)
    sel = jnp.zeros((tm, LANES), F32)
    vals, idxs = [], []
    for _ in range(TOP_K):
        m = jnp.max(l, axis=-1, keepdims=True)
        ik = jnp.min(jnp.where(l == m, lane, LANES), axis=-1, keepdims=True)
        hit = lane == ik
        sel = jnp.where(hit, 1.0, sel)
        l = jnp.where(hit, -jnp.inf, l)
        vals.append(m)
        idxs.append(ik)
    exps = [jnp.exp(v - vals[0]) for v in vals]
    tot = exps[0] + exps[1] + exps[2] + exps[3]
    r = lax.broadcasted_iota(I32, (tm, tm), 0)
    c = lax.broadcasted_iota(I32, (tm, tm), 1)
    tri = jnp.where(c < r, 1.0, 0.0).astype(BF16)
    before = jnp.dot(tri, sel.astype(BF16), preferred_element_type=F32) + carry_scr[...]
    code_out = jnp.zeros((tm, LANES), I32)
    gate_out = jnp.zeros((tm, LANES), F32)
    for k in range(TOP_K):
        pk = jnp.sum(jnp.where(lane == idxs[k], before, 0.0), axis=-1, keepdims=True)
        code_out = jnp.where(lane == k, idxs[k] * SLOT_BASE + pk.astype(I32), code_out)
        gate_out = jnp.where(lane == k, exps[k] / tot, gate_out)
    code_ref[...] = code_out
    gate_ref[...] = gate_out
    carry_scr[...] = carry_scr[...] + jnp.sum(sel, axis=0, keepdims=True)
    cnt_ref[...] = jnp.broadcast_to(carry_scr[...], cnt_ref.shape)


def _route(logits):
    n_tok = logits.shape[0]
    assert n_tok <= SLOT_BASE
    tm = ROUTE_TM
    tile = pl.BlockSpec((tm, LANES), lambda i: (i, 0))
    return pl.pallas_call(
        _route_kernel,
        out_shape=(jax.ShapeDtypeStruct((n_tok, LANES), I32),
                   jax.ShapeDtypeStruct((n_tok, LANES), F32),
                   jax.ShapeDtypeStruct((8, LANES), F32)),
        grid=(n_tok // tm,),
        in_specs=[tile],
        out_specs=(tile, tile, pl.BlockSpec((8, LANES), lambda i: (0, 0))),
        scratch_shapes=[pltpu.VMEM((1, LANES), F32)],
        compiler_params=_cparams("arbitrary"),
        name="route_topk",
    )(logits)


ZERO_ROWS = 64


DISPATCH_BUFS = 3


def _dispatch_kernel(cnt_ref, start_ref, nused_ref, row_ref, h_hbm, xs_hbm, h_buf, zero_scr, in_sems, out_sems,
                     sem, *, n_blocks):
    i = pl.program_id(0)
    last = pl.num_programs(0) - 1

    def load(tile, buf):
        return pltpu.make_async_copy(h_hbm.at[pl.ds(tile * DISPATCH_TILE, DISPATCH_TILE)], h_buf.at[buf],
                                     in_sems.at[buf])

    def row_copy(buf, src0, u, dst_row, out_sem):
        return pltpu.make_async_copy(h_buf.at[buf, pl.ds(src0, DMA_UNROLL)].at[pl.ds(u, 1)],
                                     xs_hbm.at[pl.ds(dst_row, 1)], out_sem)

    def zero_row_copy(dst_row):
        return pltpu.make_async_copy(zero_scr.at[pl.ds(0, 1)], xs_hbm.at[pl.ds(dst_row, 1)], sem)

    def zero_chunk_copy(dst_row):
        return pltpu.make_async_copy(zero_scr, xs_hbm.at[pl.ds(dst_row, ZERO_ROWS)], sem)

    @pl.when(i == 0)
    def _():
        load(0, 0).start()

    @pl.when(i < last)
    def _():
        load(i + 1, (i + 1) % DISPATCH_BUFS).start()

    buf = i % DISPATCH_BUFS
    load(i, buf).wait()

    def issue(tt, carry):
        t0 = pl.multiple_of(tt * DMA_UNROLL, DMA_UNROLL)
        for u in range(DMA_UNROLL):
            for k in range(TOP_K):
                dst = row_ref[0, 0, (t0 + u) * TOP_K + k]
                row_copy(buf, t0, u, dst, out_sems.at[i % 2]).start(priority=k % 2)
        return carry

    lax.fori_loop(0, DISPATCH_TILE // DMA_UNROLL, issue, 0)

    def drain(step_parity):
        def body(t, carry):
            for k in range(TOP_K):
                row_copy(0, 0, 0, 0, out_sems.at[step_parity]).wait()
            return carry

        lax.fori_loop(0, DISPATCH_TILE, body, 0, unroll=DMA_UNROLL)

    @pl.when(i > 0)
    def _():
        drain((i - 1) % 2)

    @pl.when(i == last)
    def _():
        drain(i % 2)
        zero_scr[...] = jnp.zeros_like(zero_scr)

        def per_expert(e, carry):
            cnt = cnt_ref[e]
            padded = (cnt + MOE_BLOCK - 1) // MOE_BLOCK * MOE_BLOCK
            base = start_ref[e]

            def z_issue(p, c):
                zero_row_copy(base + p).start()
                return c

            def z_drain(p, c):
                zero_row_copy(0).wait()
                return c

            lax.fori_loop(cnt, padded, z_issue, 0)
            lax.fori_loop(cnt, padded, z_drain, 0)
            return carry

        lax.fori_loop(0, N_EXPERTS, per_expert, 0)

        chunks = MOE_BLOCK // ZERO_ROWS

        def t_issue(c, carry):
            zero_chunk_copy(c * ZERO_ROWS).start()
            return carry

        def t_drain(c, carry):
            zero_chunk_copy(0).wait()
            return carry

        lax.fori_loop(nused_ref[0] * chunks, n_blocks * chunks, t_issue, 0)
        lax.fori_loop(nused_ref[0] * chunks, n_blocks * chunks, t_drain, 0)


def _dispatch(counts, starts, nused, rows, h, n_blocks):
    n_tok = h.shape[0]
    nt = n_tok // DISPATCH_TILE
    grid_spec = pltpu.PrefetchScalarGridSpec(
        num_scalar_prefetch=3,
        grid=(nt,),
        in_specs=[
            pl.BlockSpec((1, 1, DISPATCH_TILE * TOP_K), lambda i, *_: (i, 0, 0), memory_space=pltpu.SMEM),
            pl.BlockSpec(memory_space=pl.ANY),
        ],
        out_specs=pl.BlockSpec(memory_space=pl.ANY),
        scratch_shapes=[pltpu.VMEM((DISPATCH_BUFS, DISPATCH_TILE, PACKED_WIDTH), U32),
                        pltpu.VMEM((ZERO_ROWS, PACKED_WIDTH), U32),
                        pltpu.SemaphoreType.DMA((DISPATCH_BUFS,)), pltpu.SemaphoreType.DMA((2,)),
                        pltpu.SemaphoreType.DMA],
    )
    return pl.pallas_call(
        functools.partial(_dispatch_kernel, n_blocks=n_blocks),
        out_shape=jax.ShapeDtypeStruct((n_blocks * MOE_BLOCK, PACKED_WIDTH), U32),
        grid_spec=grid_spec,
        compiler_params=_cparams("arbitrary"),
        name="moe_dispatch",
    )(counts, starts, nused, rows.reshape(nt, 1, DISPATCH_TILE * TOP_K), h)


def _expert_kernel(blk_e_ref, next_e_ref, blk_rows_ref, nused_ref, xs_ref, bgu_ref, bd_ref, wgu_hbm, wd_hbm, ys_ref,
                   wgu_in, wd_in, wgu_scr, wd_scr, slot_ref, sems, *, layer):
    b = pl.program_id(0)
    e = blk_e_ref[b]
    prev = blk_e_ref[jnp.maximum(b - 1, 0)]

    def fetch(expert, slot):
        return (pltpu.make_async_copy(wgu_hbm.at[layer, expert], wgu_in.at[slot], sems.at[0, slot]),
                pltpu.make_async_copy(wd_hbm.at[layer, expert], wd_in.at[slot], sems.at[1, slot]))

    @pl.when(b == 0)
    def _():
        slot_ref[0] = 0
        for copy in fetch(e, 0):
            copy.start()

    @pl.when((b == 0) | (e != prev))
    def _():
        slot = slot_ref[0]
        for copy in fetch(e, slot):
            copy.wait()
        wgu_scr[...] = wgu_in[slot].astype(BF16)
        wd_scr[...] = wd_in[slot].astype(BF16)
        nxt = next_e_ref[b]

        @pl.when(nxt != e)
        def _():
            for copy in fetch(nxt, 1 - slot):
                copy.start()

        slot_ref[0] = 1 - slot

    def ffn(rows):
        x_first, x_second = _unpack_bf16_pairs(xs_ref[rows, :])
        gu = (jnp.dot(x_first, wgu_scr[0:PACKED_WIDTH, :], preferred_element_type=F32)
              + jnp.dot(x_second, wgu_scr[PACKED_WIDTH:, :], preferred_element_type=F32) + bgu_ref[0, 0])
        gate = jnp.minimum(gu[:, :D_FF], SWIGLU_LIMIT)
        up = jnp.clip(gu[:, D_FF:], -SWIGLU_LIMIT, SWIGLU_LIMIT)
        act = (up + 1) * (gate * jax.nn.sigmoid(SWIGLU_ALPHA * gate))
        ys_ref[rows, :] = jnp.dot(act.astype(BF16), wd_scr[...], preferred_element_type=F32) + bd_ref[0, 0]

    n_rows = blk_rows_ref[b]
    half = MOE_BLOCK // 2

    @pl.when(n_rows > half)
    def _():
        ffn(slice(None))

    @pl.when((n_rows > 0) & (n_rows <= half))
    def _():
        ffn(slice(0, half))
        ys_ref[half:, :] = jnp.zeros((MOE_BLOCK - half, D_MODEL), F32)

    @pl.when(n_rows == 0)
    def _():
        ys_ref[...] = jnp.zeros_like(ys_ref)


def _experts(blk_e, next_e, blk_rows, nused, xs, w_gate_up, b_gate_up, w_down, b_down, layer):
    n_blocks = blk_e.shape[0]
    depth = w_gate_up.shape[0]
    grid_spec = pltpu.PrefetchScalarGridSpec(
        num_scalar_prefetch=4,
        grid=(n_blocks,),
        in_specs=[
            pl.BlockSpec((MOE_BLOCK, PACKED_WIDTH), lambda b, be, ne, br, nu: (jnp.minimum(b, nu[0] - 1), 0)),
            pl.BlockSpec((1, 1, 1, 2 * D_FF), lambda b, be, ne, br, nu: (layer, be[b], 0, 0)),
            pl.BlockSpec((1, 1, 1, D_MODEL), lambda b, be, ne, br, nu: (layer, be[b], 0, 0)),
            pl.BlockSpec(memory_space=pl.ANY),
            pl.BlockSpec(memory_space=pl.ANY),
        ],
        out_specs=pl.BlockSpec((MOE_BLOCK, D_MODEL), lambda b, be, ne, br, nu: (b, 0)),
        scratch_shapes=[pltpu.VMEM((2, D_MODEL, 2 * D_FF), F32), pltpu.VMEM((2, D_FF, D_MODEL), F32),
                        pltpu.VMEM((D_MODEL, 2 * D_FF), BF16), pltpu.VMEM((D_FF, D_MODEL), BF16),
                        pltpu.SMEM((1,), I32), pltpu.SemaphoreType.DMA((2, 2))],
    )
    return pl.pallas_call(
        functools.partial(_expert_kernel, layer=layer),
        out_shape=jax.ShapeDtypeStruct((xs.shape[0], D_MODEL), F32),
        grid_spec=grid_spec,
        compiler_params=_cparams("arbitrary"),
        name="moe_experts",
    )(blk_e, next_e, blk_rows, nused, xs, b_gate_up.reshape(depth, N_EXPERTS, 1, 2 * D_FF),
      b_down.reshape(depth, N_EXPERTS, 1, D_MODEL), w_gate_up, w_down)


def _combine_kernel(row_ref, next_row_ref, gate_ref, x_ref, mgate_ref, gfin_ref, ys_hbm, *out_and_scratch,
                    n_ctx, final_norm):
    *out_refs, rows_scr, sems = out_and_scratch
    i = pl.program_id(0)
    n_steps = pl.num_programs(0)
    crow = _cond_row(i, TOKEN_TILE, n_ctx)
    slot = i % 2

    def row_copy(buf, src_row, k, t0, u):
        return pltpu.make_async_copy(ys_hbm.at[pl.ds(src_row, 1)],
                                     rows_scr.at[buf, k, pl.ds(t0, DMA_UNROLL)].at[pl.ds(u, 1)],
                                     sems.at[buf])

    def gather(rows, buf):
        def issue(tt, carry):
            t0 = pl.multiple_of(tt * DMA_UNROLL, DMA_UNROLL)
            for u in range(DMA_UNROLL):
                for k in range(TOP_K):
                    row_copy(buf, rows[0, 0, (t0 + u) * TOP_K + k], k, t0, u).start(priority=k % 2)
            return carry

        lax.fori_loop(0, TOKEN_TILE // DMA_UNROLL, issue, 0)

    @pl.when(i == 0)
    def _():
        gather(row_ref, 0)

    @pl.when(i + 1 < n_steps)
    def _():
        gather(next_row_ref, 1 - slot)

    def drain(t, carry):
        for k in range(TOP_K):
            row_copy(slot, 0, 0, 0, 0).wait()
        return carry

    lax.fori_loop(0, TOKEN_TILE, drain, 0, unroll=DMA_UNROLL)

    def consume(o_ref):
        g = gate_ref[...]
        y = g[:, 0:1] * rows_scr[slot, 0]
        for k in range(1, TOP_K):
            y = y + g[:, k:k + 1] * rows_scr[slot, k]
        x = x_ref[...] + mgate_ref[0, 0, pl.ds(crow, 1), :] * y
        if final_norm:
            x = x * lax.rsqrt(jnp.mean(x * x, axis=-1, keepdims=True) + EPS) * gfin_ref[...]
        o_ref[...] = x

    if len(out_refs) == 1:
        consume(out_refs[0])
    else:
        ctx_ref, lat_ref = out_refs
        is_ctx = i * TOKEN_TILE < n_ctx
        pl.when(is_ctx)(lambda: consume(ctx_ref))
        pl.when(jnp.logical_not(is_ctx))(lambda: consume(lat_ref))


def _combine(rows, gates, x, mod, layer, ys, g_final, n_ctx, final_norm):
    n_tok = x.shape[0]
    nt = n_tok // TOKEN_TILE
    ctx_tiles = n_ctx // TOKEN_TILE
    row_block = (1, 1, TOKEN_TILE * TOP_K)
    rows = rows.reshape(nt, 1, TOKEN_TILE * TOP_K)
    tile = (TOKEN_TILE, D_MODEL)
    if final_norm:
        out_shape = (jax.ShapeDtypeStruct((n_ctx, D_MODEL), F32), jax.ShapeDtypeStruct((n_tok - n_ctx, D_MODEL), F32))
        out_specs = (pl.BlockSpec(tile, lambda i: (jnp.minimum(i, ctx_tiles - 1), 0)),
                     pl.BlockSpec(tile, lambda i: (jnp.maximum(i - ctx_tiles, 0), 0)))
    else:
        out_shape = jax.ShapeDtypeStruct((n_tok, D_MODEL), F32)
        out_specs = pl.BlockSpec(tile, lambda i: (i, 0))
    return pl.pallas_call(
        functools.partial(_combine_kernel, n_ctx=n_ctx, final_norm=final_norm),
        out_shape=out_shape,
        grid=(nt,),
        in_specs=[
            pl.BlockSpec(row_block, lambda i: (i, 0, 0), memory_space=pltpu.SMEM),
            pl.BlockSpec(row_block, lambda i: (jnp.minimum(i + 1, nt - 1), 0, 0), memory_space=pltpu.SMEM),
            pl.BlockSpec((TOKEN_TILE, LANES), lambda i: (i, 0)),
            pl.BlockSpec(tile, lambda i: (i, 0)),
            pl.BlockSpec((1, 1, COND_ROWS, D_MODEL), lambda i: (layer, 5, 0, 0)),
            pl.BlockSpec((1, D_MODEL), lambda i: (0, 0)),
            pl.BlockSpec(memory_space=pl.ANY),
        ],
        out_specs=out_specs,
        scratch_shapes=[pltpu.VMEM((2, TOP_K, TOKEN_TILE, D_MODEL), F32), pltpu.SemaphoreType.DMA((2,))],
        compiler_params=_cparams("arbitrary"),
        name="moe_combine",
    )(rows, rows, gates, x, mod, g_final, ys)


def _rope_tables():
    t = np.arange(LAT_SEQ)
    row = (t // GRID_W).astype(np.float32)
    col = (t % GRID_W).astype(np.float32)
    half = HEAD_DIM // 2
    inv = jnp.asarray(ROPE_THETA, F32) ** (-jnp.arange(0, half, 2, dtype=F32) / half)
    ang = jnp.concatenate([row[:, None] * inv, col[:, None] * inv], axis=-1)
    cos, sin = jnp.cos(ang), jnp.sin(ang)
    return jnp.concatenate([cos, cos], axis=-1), jnp.concatenate([-sin, sin], axis=-1)


def _slot_rows(codes, starts):
    expert = lax.shift_right_logical(codes, SLOT_SHIFT)
    base = jnp.sum(jnp.where(expert[..., None] == jnp.arange(N_EXPERTS, dtype=I32), starts, 0), axis=-1)
    return base + (codes & (SLOT_BASE - 1))


def _routing_plan(counts_f, n_blocks):
    counts = counts_f[0, :N_EXPERTS].astype(I32)
    padded = (counts + MOE_BLOCK - 1) // MOE_BLOCK * MOE_BLOCK
    pend = jnp.cumsum(padded)
    starts = pend - padded
    nused = pend[-1] // MOE_BLOCK
    blk = jnp.minimum(jnp.arange(n_blocks, dtype=I32), nused - 1)
    blk_e = jnp.sum((pend[None, :] <= (blk * MOE_BLOCK)[:, None]).astype(I32), axis=1)
    blk_e = jnp.minimum(blk_e, N_EXPERTS - 1)
    ids = jnp.arange(N_EXPERTS, dtype=I32)
    later = (ids[None, :] > blk_e[:, None]) & (counts > 0)[None, :]
    next_e = jnp.min(jnp.where(later, ids[None, :], N_EXPERTS), axis=1)
    next_e = jnp.where(next_e == N_EXPERTS, blk_e, next_e).astype(I32)
    mine = ids[None, :] == blk_e[:, None]
    group_end = jnp.sum(jnp.where(mine, (starts + counts)[None, :], 0), axis=1)
    first_row = jnp.arange(n_blocks, dtype=I32) * MOE_BLOCK
    blk_rows = jnp.where(jnp.arange(n_blocks) < nused, jnp.clip(group_end - first_row, 0, MOE_BLOCK), 0).astype(I32)
    return counts, starts.astype(I32), nused.reshape(1).astype(I32), blk_e, next_e, blk_rows


def _forward(x_prompt, x_sample, cache_na_k, cache_na_v, cache_gqa_k, cache_gqa_v, c, c_ctx,
             w_ada, b_ada, g_norm1, g_norm2, w_in, w_pool, pool_scale, na_rel_bias, g_q, g_k,
             w_out, w_router, b_router, w_gate_up, b_gate_up, w_down, b_down, g_final):
    depth = w_in.shape[0]
    nb_ctx, nb_lat = x_prompt.shape[0], x_sample.shape[0]
    n_ctx, n_lat = nb_ctx * CTX_SEQ, nb_lat * LAT_SEQ
    n_tok = n_ctx + n_lat
    assert x_prompt.shape[1:] == (CTX_SEQ, D_MODEL) and x_sample.shape[1:] == (LAT_SEQ, D_MODEL)
    assert n_ctx % LAT_SEQ == 0 and 1 + nb_lat <= COND_ROWS
    n_blocks = n_tok * TOP_K // MOE_BLOCK + N_EXPERTS

    stream = (x_prompt.reshape(n_ctx, D_MODEL), x_sample.reshape(n_lat, D_MODEL))
    cond = jnp.zeros((COND_ROWS, D_MODEL), F32).at[0].set(c_ctx).at[1:1 + nb_lat].set(c)
    mod = _ada_mod(cond, w_ada, b_ada)

    cos2, sin2 = _rope_tables()
    na_bias = _na_bias_tables(na_rel_bias)
    w_in_b = w_in.astype(BF16)
    w_out_b = w_out.astype(BF16)
    w_pool_b = w_pool.astype(BF16)
    w_router_b = jnp.pad(w_router, ((0, 0), (0, 0), (0, LANES - N_EXPERTS))).astype(BF16)
    b_router_p = jnp.pad(b_router, ((0, 0), (0, LANES - N_EXPERTS)), constant_values=NEG)
    cna_k = cache_na_k.reshape(nb_lat, depth, CTX_SEQ, NA_WIDTH).astype(BF16)
    cna_v = cache_na_v.reshape(nb_lat, depth, CTX_SEQ, NA_WIDTH).astype(BF16)
    cgq_k = cache_gqa_k.reshape(nb_lat, depth, CTX_SEQ, GQA_KV_WIDTH).astype(BF16)
    cgq_v = cache_gqa_v.reshape(nb_lat, depth, CTX_SEQ, GQA_KV_WIDTH).astype(BF16)

    new_kv = []
    for l in range(depth):
        u, kv_f, proj_b = _inproj(stream, mod, l, g_norm1[l][None], w_in_b, g_q[l][None], g_k[l][None],
                                  cos2, sin2, n_ctx)
        new_kv.append(kv_f[:n_ctx])
        ya = _pool(u, w_pool_b[l], pool_scale[l][None], n_ctx)
        yb = _na_attention(proj_b, cna_k, cna_v, na_bias, l, n_ctx)
        yc = _gqa_attention(proj_b, cgq_k, cgq_v, l, n_ctx)
        x, h, logits = _outproj(stream, ya, yb, yc, w_out_b, mod, l, g_norm2[l][None],
                                w_router_b[l], b_router_p[l][None], n_ctx)
        codes, gates, counts_f = _route(logits)
        counts, starts, nused, blk_e, next_e, blk_rows = _routing_plan(counts_f, n_blocks)
        rows = _slot_rows(codes[:, :TOP_K], starts)
        xs = _dispatch(counts, starts, nused, rows, h, n_blocks)
        ys = _experts(blk_e, next_e, blk_rows, nused, xs, w_gate_up, b_gate_up, w_down, b_down, l)
        x = _combine(rows, gates, x, mod, l, ys, g_final[None], n_ctx, final_norm=(l == depth - 1))
        stream = (x,)

    y_ctx, y_lat = x
    y_prompt = y_ctx.reshape(nb_ctx, CTX_SEQ, D_MODEL)
    y_sample = y_lat.reshape(nb_lat, LAT_SEQ, D_MODEL)

    def stack(col, width):
        per_layer = [p[:, col:col + width].reshape(nb_ctx, CTX_SEQ, width // HEAD_DIM, HEAD_DIM) for p in new_kv]
        return jnp.stack(per_layer, axis=1)

    return (y_prompt, y_sample, stack(KVF_KN, NA_WIDTH), stack(KVF_VN, NA_WIDTH),
            stack(KVF_KG, GQA_KV_WIDTH), stack(KVF_VG, GQA_KV_WIDTH))


def kernel(x_prompt, x_sample, cache_na_k, cache_na_v, cache_gqa_k, cache_gqa_v, c, c_ctx, w_ada, b_ada,
           g_norm1, g_norm2, w_in, w_pool, pool_scale, na_rel_bias, g_q, g_k, w_out, w_router, b_router,
           w_gate_up, b_gate_up, w_down, b_down, g_final):
    return _forward(x_prompt, x_sample, cache_na_k, cache_na_v, cache_gqa_k, cache_gqa_v, c, c_ctx,
                    w_ada, b_ada, g_norm1, g_norm2, w_in, w_pool, pool_scale, na_rel_bias, g_q, g_k,
                    w_out, w_router, b_router, w_gate_up, b_gate_up, w_down, b_down, g_final)
```

```python
import functools
import math

import numpy as np
import jax
import jax.numpy as jnp
from jax import lax
from jax.experimental import pallas as pl
from jax.experimental.pallas import tpu as pltpu

F32 = jnp.float32
BF16 = jnp.bfloat16
I32 = jnp.int32

D_MODEL = 2048
HEAD_DIM = 128
CTX_SEQ = 256
LAT_SEQ = 2048
GRID_W = 64
GRID_ROWS = LAT_SEQ // GRID_W
POOL_WINDOWS = (2, 4, 8, 16)
POOL_GROUP = 128
POOL_WIDTH = 512
NA_HEADS = 4
NA_WIDTH = 512
NA_ROWS = 8
NA_COLS = 16
GQA_HEADS = 8
GQA_KV_HEADS = 2
GQA_GROUP = GQA_HEADS // GQA_KV_HEADS
GQA_WIDTH = 1024
GQA_KV_WIDTH = 256
IN_WIDTH = 3584
N_EXPERTS = 32
TOP_K = 4
D_FF = 512
SWIGLU_LIMIT = 7.0
SWIGLU_ALPHA = 1.702
ROPE_THETA = 10000.0
EPS = 1e-6
NEG = -1e30

COL_U, COL_QN, COL_KN, COL_VN, COL_QG, COL_KG, COL_VG = 0, 512, 1024, 1536, 2048, 3072, 3328
ACT_QG, ACT_QN, ACT_KN, ACT_VN, ACT_KG, ACT_VG, ACT_WIDTH = 0, 1024, 1536, 2048, 2560, 2816, 3072
KVF_KN, KVF_VN, KVF_KG, KVF_VG, KVF_WIDTH = 0, 512, 1024, 1280, 1536

LANES = 128
COND_ROWS = 16
ATT_TILE = 256
NA_TILE_ROWS = ATT_TILE // GRID_W
NA_WIN_ROWS = NA_ROWS + NA_TILE_ROWS - 1
NA_WIN = NA_WIN_ROWS * GRID_W
INPROJ_TM = 512
OUTPROJ_TM = 512
ROUTE_TM = 512
MOE_BLOCK = 512
TOKEN_TILE = 256
DISPATCH_TILE = 512
SLOT_SHIFT = 16
SLOT_BASE = 1 << SLOT_SHIFT
DMA_UNROLL = 16
VMEM_LIMIT = 56 * 1024 * 1024


def _cparams(*sem):
    return pltpu.CompilerParams(dimension_semantics=sem, vmem_limit_bytes=VMEM_LIMIT)


def _cond_row(tile, tile_rows, n_ctx):
    row0 = tile * tile_rows
    return jnp.where(row0 < n_ctx, 0, 1 + (row0 - n_ctx) // LAT_SEQ)


def _head(h):
    return slice(h * HEAD_DIM, (h + 1) * HEAD_DIM)


def _stream_specs(stream, tm, tile_of_step, n_ctx):
    if len(stream) == 1:
        return [pl.BlockSpec((tm, D_MODEL), lambda i: (tile_of_step(i), 0))]
    ctx_tiles = n_ctx // tm
    return [pl.BlockSpec((tm, D_MODEL), lambda i: (jnp.minimum(tile_of_step(i), ctx_tiles - 1), 0)),
            pl.BlockSpec((tm, D_MODEL), lambda i: (jnp.maximum(tile_of_step(i) - ctx_tiles, 0), 0))]


def _stream_rows(refs, tile, tm, n_ctx, rows):
    if len(refs) == 1:
        return refs[0][rows, :]
    return jnp.where(tile * tm < n_ctx, refs[0][rows, :], refs[1][rows, :])


U32 = jnp.uint32
PACKED_WIDTH = D_MODEL // 2


def _pack_bf16_pairs(x):
    half = x.shape[1] // 2
    hi = pltpu.bitcast(x[:, :half].astype(BF16).astype(F32), U32)
    lo = pltpu.bitcast(x[:, half:].astype(BF16).astype(F32), U32)
    return hi | (lo >> 16)


def _unpack_bf16_pairs(p):
    first = pltpu.bitcast(p & jnp.uint32(0xFFFF0000), F32).astype(BF16)
    second = pltpu.bitcast(p << 16, F32).astype(BF16)
    return first, second


def _ada_kernel(c_ref, w_ref, b_ref, o_ref):
    c = c_ref[...]
    s = (c * jax.nn.sigmoid(c)).astype(BF16)
    o_ref[0, 0] = jnp.dot(s, w_ref[0].astype(BF16), preferred_element_type=F32) + b_ref[0]


def _ada_mod(cond, w_ada, b_ada):
    depth = w_ada.shape[0]
    tn = 1024
    per = D_MODEL // tn
    return pl.pallas_call(
        _ada_kernel,
        out_shape=jax.ShapeDtypeStruct((depth, 6, COND_ROWS, D_MODEL), F32),
        grid=(depth, 6 * per),
        in_specs=[
            pl.BlockSpec((COND_ROWS, D_MODEL), lambda l, j: (0, 0)),
            pl.BlockSpec((1, D_MODEL, tn), lambda l, j: (l, 0, j)),
            pl.BlockSpec((1, 1, tn), lambda l, j: (l, 0, j)),
        ],
        out_specs=pl.BlockSpec((1, 1, COND_ROWS, tn), lambda l, j: (l, j // per, 0, j % per)),
        compiler_params=_cparams("arbitrary", "arbitrary"),
        name="ada_mod",
    )(cond, w_ada, b_ada.reshape(depth, 1, 6 * D_MODEL))


def _head_rms(x, g):
    return x * lax.rsqrt(jnp.mean(x * x, axis=-1, keepdims=True) + EPS) * g


NORM_CHUNKS = 8


def _inproj_kernel(*refs, n_ctx, n_tiles, n_stream):
    xn_refs, refs = refs[:n_stream], refs[n_stream:]
    (shift_ref, scale_ref, g1_ref, w_ref, gq_ref, gk_ref, cos_ref, sin_ref,
     u_ref, kv_ref, act_ref, h_even, h_odd) = refs
    i = pl.program_id(0)
    nxt = jnp.minimum(i, n_tiles - 1)
    is_lat = jnp.maximum(i - 1, 0) * INPROJ_TM >= n_ctx
    cos = jnp.where(is_lat, cos_ref[...], 1.0)
    sin = jnp.where(is_lat, sin_ref[...], 0.0)

    def normalize(x_refs, tile, h_ref, rows):
        crow = _cond_row(tile, INPROJ_TM, n_ctx)
        x = _stream_rows(x_refs, tile, INPROJ_TM, n_ctx, rows)
        y = x * lax.rsqrt(jnp.mean(x * x, axis=-1, keepdims=True) + EPS) * g1_ref[...]
        h_ref[rows, :] = (y * (1 + scale_ref[0, 0, pl.ds(crow, 1), :])
                          + shift_ref[0, 0, pl.ds(crow, 1), :]).astype(BF16)

    def normed_rope(xh, g_ref):
        xh = _head_rms(xh, g_ref[...])
        return xh * cos + pltpu.roll(xh, HEAD_DIM // 2, 1) * sin

    def project(h_ref, h_next_ref):
        chunk = INPROJ_TM // NORM_CHUNKS
        steps = iter(range(NORM_CHUNKS))

        def proj(col, width):
            acc = jnp.dot(h_ref[...], w_ref[0, :, col:col + width], preferred_element_type=F32)
            c = next(steps)
            normalize(xn_refs, nxt, h_next_ref, slice(c * chunk, (c + 1) * chunk))
            return acc

        for half in range(GQA_KV_HEADS):
            q = proj(COL_QG + half * GQA_GROUP * HEAD_DIM, GQA_GROUP * HEAD_DIM)
            for g in range(GQA_GROUP):
                col = ACT_QG + (half * GQA_GROUP + g) * HEAD_DIM
                act_ref[:, col:col + HEAD_DIM] = normed_rope(q[:, _head(g)], gq_ref).astype(BF16)
        kg = proj(COL_KG, GQA_KV_WIDTH)
        for kh in range(GQA_KV_HEADS):
            v = normed_rope(kg[:, _head(kh)], gk_ref)
            kv_ref[:, KVF_KG + kh * HEAD_DIM:KVF_KG + (kh + 1) * HEAD_DIM] = v
            act_ref[:, ACT_KG + kh * HEAD_DIM:ACT_KG + (kh + 1) * HEAD_DIM] = v.astype(BF16)
        vg = proj(COL_VG, GQA_KV_WIDTH)
        kv_ref[:, KVF_VG:KVF_VG + GQA_KV_WIDTH] = vg
        act_ref[:, ACT_VG:ACT_VG + GQA_KV_WIDTH] = vg.astype(BF16)
        for col, kv_col, act_col in ((COL_KN, KVF_KN, ACT_KN), (COL_VN, KVF_VN, ACT_VN)):
            v = proj(col, NA_WIDTH)
            kv_ref[:, kv_col:kv_col + NA_WIDTH] = v
            act_ref[:, act_col:act_col + NA_WIDTH] = v.astype(BF16)
        act_ref[:, ACT_QN:ACT_QN + NA_WIDTH] = proj(COL_QN, NA_WIDTH).astype(BF16)
        u_ref[...] = proj(COL_U, POOL_WIDTH)

    @pl.when(i == 0)
    def _():
        normalize(xn_refs, 0, h_even, slice(None))

    @pl.when(i % 2 == 1)
    def _():
        project(h_even, h_odd)

    @pl.when((i > 0) & (i % 2 == 0))
    def _():
        project(h_odd, h_even)


def _inproj(stream, mod, layer, g1, w_in_b, g_q, g_k, cos2, sin2, n_ctx):
    n_tok = sum(a.shape[0] for a in stream)
    tm = INPROJ_TM
    lat_tiles = LAT_SEQ // tm
    ctx_tiles = n_ctx // tm
    n_tiles = n_tok // tm

    def rope_map(i):
        return (jnp.maximum(i - 1 - ctx_tiles, 0) % lat_tiles, 0)

    def rows(width):
        return pl.BlockSpec((tm, width), lambda i: (jnp.maximum(i - 1, 0), 0))

    return pl.pallas_call(
        functools.partial(_inproj_kernel, n_ctx=n_ctx, n_tiles=n_tiles, n_stream=len(stream)),
        out_shape=(jax.ShapeDtypeStruct((n_tok, POOL_WIDTH), F32),
                   jax.ShapeDtypeStruct((n_tok, KVF_WIDTH), F32),
                   jax.ShapeDtypeStruct((n_tok, ACT_WIDTH), BF16)),
        grid=(n_tiles + 1,),
        in_specs=[
            *_stream_specs(stream, tm, lambda i: jnp.minimum(i, n_tiles - 1), n_ctx),
            pl.BlockSpec((1, 1, COND_ROWS, D_MODEL), lambda i: (layer, 0, 0, 0)),
            pl.BlockSpec((1, 1, COND_ROWS, D_MODEL), lambda i: (layer, 1, 0, 0)),
            pl.BlockSpec((1, D_MODEL), lambda i: (0, 0)),
            pl.BlockSpec((1, D_MODEL, IN_WIDTH), lambda i: (layer, 0, 0), pipeline_mode=pl.Buffered(1)),
            pl.BlockSpec((1, HEAD_DIM), lambda i: (0, 0)),
            pl.BlockSpec((1, HEAD_DIM), lambda i: (0, 0)),
            pl.BlockSpec((tm, HEAD_DIM), rope_map),
            pl.BlockSpec((tm, HEAD_DIM), rope_map),
        ],
        out_specs=(rows(POOL_WIDTH), rows(KVF_WIDTH), rows(ACT_WIDTH)),
        scratch_shapes=[pltpu.VMEM((tm, D_MODEL), BF16), pltpu.VMEM((tm, D_MODEL), BF16)],
        compiler_params=_cparams("arbitrary"),
        name="norm_inproj",
    )(*stream, mod, mod, g1, w_in_b, g_q, g_k, cos2, sin2)


POOL_HALO = 8


def _pool_kernel(up_ref, uc_ref, un_ref, wp_ref, ps_ref, o_ref, pad_scr, *, n_ctx):
    i = pl.program_id(0)
    row0 = i * ATT_TILE
    is_ctx = row0 < n_ctx
    t0 = jnp.where(is_ctx, 0, (row0 - n_ctx) % LAT_SEQ)
    seq = jnp.where(is_ctx, CTX_SEQ, LAT_SEQ)
    first = t0 == 0
    last = t0 + ATT_TILE == seq
    zeros = jnp.zeros((POOL_HALO, POOL_WIDTH), F32)
    pad_scr[0:POOL_HALO, :] = jnp.where(first, zeros, up_ref[ATT_TILE - POOL_HALO:ATT_TILE, :])
    pad_scr[POOL_HALO:POOL_HALO + ATT_TILE, :] = uc_ref[...]
    pad_scr[POOL_HALO + ATT_TILE:, :] = jnp.where(last, zeros, un_ref[0:POOL_HALO, :])
    t = t0 + lax.broadcasted_iota(I32, (ATT_TILE, 1), 0)
    for g, w in enumerate(POOL_WINDOWS):
        cols = slice(g * POOL_GROUP, (g + 1) * POOL_GROUP)
        acc = pad_scr[pl.ds(POOL_HALO - w // 2, ATT_TILE), cols]
        for d in range(-w // 2 + 1, w // 2):
            acc = acc + pad_scr[pl.ds(POOL_HALO + d, ATT_TILE), cols]
        cnt = (jnp.minimum(t + w // 2, seq) - jnp.maximum(t - w // 2, 0)).astype(F32)
        pooled = acc / cnt - uc_ref[:, cols]
        y = jnp.dot(pooled.astype(BF16), wp_ref[g], preferred_element_type=F32)
        o_ref[:, cols] = (y * ps_ref[:, cols]).astype(BF16)


def _pool(proj_f, w_pool_b, pool_scale, n_ctx):
    n_tok = proj_f.shape[0]
    nt = n_tok // ATT_TILE
    return pl.pallas_call(
        functools.partial(_pool_kernel, n_ctx=n_ctx),
        out_shape=jax.ShapeDtypeStruct((n_tok, POOL_WIDTH), BF16),
        grid=(nt,),
        in_specs=[
            pl.BlockSpec((ATT_TILE, POOL_WIDTH), lambda i: (jnp.maximum(i - 1, 0), 0)),
            pl.BlockSpec((ATT_TILE, POOL_WIDTH), lambda i: (i, 0)),
            pl.BlockSpec((ATT_TILE, POOL_WIDTH), lambda i: (jnp.minimum(i + 1, nt - 1), 0)),
            pl.BlockSpec((len(POOL_WINDOWS), POOL_GROUP, POOL_GROUP), lambda i: (0, 0, 0)),
            pl.BlockSpec((1, POOL_WIDTH), lambda i: (0, 0)),
        ],
        out_specs=pl.BlockSpec((ATT_TILE, POOL_WIDTH), lambda i: (i, 0)),
        scratch_shapes=[pltpu.VMEM((ATT_TILE + 2 * POOL_HALO, POOL_WIDTH), F32)],
        compiler_params=_cparams("arbitrary"),
        name="pool_mixer",
    )(proj_f, proj_f, proj_f, w_pool_b, pool_scale)


_NT_DIMS = (((1,), (1,)), ((), ()))
ATT_SCALE = HEAD_DIM ** -0.5
EXP2_SCALE = ATT_SCALE * math.log2(math.e)


def _scores(q, k):
    return lax.dot_general(q, k, _NT_DIMS, preferred_element_type=F32)


def _attend_one(q, k, v):
    s = _scores(q, k)
    p = jnp.exp2((s - jnp.max(s, axis=-1, keepdims=True)) * EXP2_SCALE)
    o = jnp.dot(p.astype(BF16), v, preferred_element_type=F32)
    return o / jnp.sum(p, axis=-1, keepdims=True)


def _attend_two(s1, v1, s2, v2):
    m = jnp.maximum(jnp.max(s1, axis=-1, keepdims=True), jnp.max(s2, axis=-1, keepdims=True))
    p1 = jnp.exp2((s1 - m) * EXP2_SCALE)
    p2 = jnp.exp2((s2 - m) * EXP2_SCALE)
    o = (jnp.dot(p1.astype(BF16), v1, preferred_element_type=F32)
         + jnp.dot(p2.astype(BF16), v2, preferred_element_type=F32))
    return o / (jnp.sum(p1, axis=-1, keepdims=True) + jnp.sum(p2, axis=-1, keepdims=True))


def _na_kernel(q_ref, kc_ref, vc_ref, kl_ref, vl_ref, ck_ref, cv_ref, bias_ref, o_ref, *, n_ctx_steps):
    i = pl.program_id(0)

    @pl.when(i < n_ctx_steps)
    def _():
        for h in range(NA_HEADS):
            o = _attend_one(q_ref[:, _head(h)], kc_ref[:, _head(h)], vc_ref[:, _head(h)])
            o_ref[:, _head(h)] = o.astype(BF16)

    @pl.when(i >= n_ctx_steps)
    def _():
        tile = (i - n_ctx_steps) % (LAT_SEQ // ATT_TILE)
        win_row = jnp.clip(tile * NA_TILE_ROWS - NA_ROWS // 2, 0, GRID_ROWS - NA_WIN_ROWS)
        key0 = pl.multiple_of(win_row * GRID_W, GRID_W)
        def scores(h):
            q = q_ref[:, _head(h)]
            return (_scores(q, kl_ref[pl.ds(key0, NA_WIN), _head(h)]) + bias_ref[0, 0, h],
                    _scores(q, ck_ref[0, 0, :, _head(h)]))

        nxt = scores(0)
        for h in range(NA_HEADS):
            s_lat, s_ctx = nxt
            if h + 1 < NA_HEADS:
                nxt = scores(h + 1)
            o = _attend_two(s_lat, vl_ref[pl.ds(key0, NA_WIN), _head(h)],
                            s_ctx, cv_ref[0, 0, :, _head(h)])
            o_ref[:, _head(h)] = o.astype(BF16)


def _na_bias_tables(rel_bias):
    depth = rel_bias.shape[0]
    half = NA_ROWS // 2
    r0 = np.array([0, half, GRID_ROWS - NA_TILE_ROWS])[:, None, None]
    win = np.clip(r0 - half, 0, GRID_ROWS - NA_WIN_ROWS)
    qr = r0 + np.arange(NA_TILE_ROWS)[None, :, None]
    kr = win + np.arange(NA_WIN_ROWS)[None, None, :]
    rs = np.clip(qr - half, 0, GRID_ROWS - NA_ROWS)
    row_ok = (kr >= rs) & (kr < rs + NA_ROWS)
    dr = np.clip(kr - qr + NA_ROWS - 1, 0, 2 * NA_ROWS - 2)
    qc = np.arange(GRID_W)[:, None]
    kc = np.arange(GRID_W)[None, :]
    cs = np.clip(qc - NA_COLS // 2, 0, GRID_W - NA_COLS)
    col_ok = (kc >= cs) & (kc < cs + NA_COLS)
    dc = np.clip(kc - qc + NA_COLS - 1, 0, 2 * NA_COLS - 2)
    sel_r = (dr[..., None] == np.arange(2 * NA_ROWS - 1)).astype(np.float32)
    sel_c = (dc[..., None] == np.arange(2 * NA_COLS - 1)).astype(np.float32)
    vals = jnp.einsum("sqka,lhab,xyb->lshqxky", sel_r, rel_bias, sel_c, precision=lax.Precision.HIGHEST)
    ok = row_ok[:, :, None, :, None] & col_ok[None, None, :, None, :]
    table = jnp.where(ok[None, :, None], vals / ATT_SCALE, NEG)
    return table.reshape(depth, 3, NA_HEADS, ATT_TILE, NA_WIN).astype(F32)


def _na_attention(proj_b, cache_k, cache_v, bias, layer, n_ctx):
    n_tok = proj_b.shape[0]
    n_ctx_steps = n_ctx // ATT_TILE
    lat_tiles = LAT_SEQ // ATT_TILE
    lat_blk0 = n_ctx // LAT_SEQ
    wq = NA_WIDTH

    def ctx_blk(i):
        return jnp.minimum(i, n_ctx_steps - 1)

    def req(i):
        return jnp.maximum(i - n_ctx_steps, 0) // lat_tiles

    def kind(i):
        tile = jnp.maximum(i - n_ctx_steps, 0) % lat_tiles
        return jnp.where(tile == 0, 0, jnp.where(tile == lat_tiles - 1, 2, 1))

    return pl.pallas_call(
        functools.partial(_na_kernel, n_ctx_steps=n_ctx_steps),
        out_shape=jax.ShapeDtypeStruct((n_tok, NA_WIDTH), BF16),
        grid=(n_tok // ATT_TILE,),
        in_specs=[
            pl.BlockSpec((ATT_TILE, wq), lambda i: (i, ACT_QN // wq)),
            pl.BlockSpec((ATT_TILE, wq), lambda i: (ctx_blk(i), ACT_KN // wq)),
            pl.BlockSpec((ATT_TILE, wq), lambda i: (ctx_blk(i), ACT_VN // wq)),
            pl.BlockSpec((LAT_SEQ, wq), lambda i: (lat_blk0 + req(i), ACT_KN // wq)),
            pl.BlockSpec((LAT_SEQ, wq), lambda i: (lat_blk0 + req(i), ACT_VN // wq)),
            pl.BlockSpec((1, 1, CTX_SEQ, wq), lambda i: (req(i), layer, 0, 0)),
            pl.BlockSpec((1, 1, CTX_SEQ, wq), lambda i: (req(i), layer, 0, 0)),
            pl.BlockSpec((1, 1, NA_HEADS, ATT_TILE, NA_WIN), lambda i: (layer, kind(i), 0, 0, 0)),
        ],
        out_specs=pl.BlockSpec((ATT_TILE, wq), lambda i: (i, 0)),
        compiler_params=_cparams("arbitrary"),
        name="na_attention",
    )(proj_b, proj_b, proj_b, proj_b, proj_b, cache_k, cache_v, bias)


def _gqa_kernel(q_ref, kc_ref, vc_ref, kl_ref, vl_ref, ck_ref, cv_ref, o_ref, *, n_ctx_steps):
    i = pl.program_id(0)

    def kv_head(h):
        return _head(h // GQA_GROUP)

    @pl.when(i < n_ctx_steps)
    def _():
        for h in range(GQA_HEADS):
            o = _attend_one(q_ref[:, _head(h)], kc_ref[:, kv_head(h)], vc_ref[:, kv_head(h)])
            o_ref[:, _head(h)] = o.astype(BF16)

    @pl.when(i >= n_ctx_steps)
    def _():
        def scores(h):
            q = q_ref[:, _head(h)]
            return _scores(q, kl_ref[:, kv_head(h)]), _scores(q, ck_ref[0, 0, :, kv_head(h)])

        nxt = scores(0)
        for h in range(GQA_HEADS):
            s_lat, s_ctx = nxt
            if h + 1 < GQA_HEADS:
                nxt = scores(h + 1)
            o = _attend_two(s_lat, vl_ref[:, kv_head(h)], s_ctx, cv_ref[0, 0, :, kv_head(h)])
            o_ref[:, _head(h)] = o.astype(BF16)


def _gqa_attention(proj_b, cache_k, cache_v, layer, n_ctx):
    n_tok = proj_b.shape[0]
    n_ctx_steps = n_ctx // ATT_TILE
    lat_tiles = LAT_SEQ // ATT_TILE
    lat_blk0 = n_ctx // LAT_SEQ
    wkv = GQA_KV_WIDTH

    def ctx_blk(i):
        return jnp.minimum(i, n_ctx_steps - 1)

    def req(i):
        return jnp.maximum(i - n_ctx_steps, 0) // lat_tiles

    return pl.pallas_call(
        functools.partial(_gqa_kernel, n_ctx_steps=n_ctx_steps),
        out_shape=jax.ShapeDtypeStruct((n_tok, GQA_WIDTH), BF16),
        grid=(n_tok // ATT_TILE,),
        in_specs=[
            pl.BlockSpec((ATT_TILE, GQA_WIDTH), lambda i: (i, ACT_QG // GQA_WIDTH)),
            pl.BlockSpec((ATT_TILE, wkv), lambda i: (ctx_blk(i), ACT_KG // wkv)),
            pl.BlockSpec((ATT_TILE, wkv), lambda i: (ctx_blk(i), ACT_VG // wkv)),
            pl.BlockSpec((LAT_SEQ, wkv), lambda i: (lat_blk0 + req(i), ACT_KG // wkv)),
            pl.BlockSpec((LAT_SEQ, wkv), lambda i: (lat_blk0 + req(i), ACT_VG // wkv)),
            pl.BlockSpec((1, 1, CTX_SEQ, wkv), lambda i: (req(i), layer, 0, 0)),
            pl.BlockSpec((1, 1, CTX_SEQ, wkv), lambda i: (req(i), layer, 0, 0)),
        ],
        out_specs=pl.BlockSpec((ATT_TILE, GQA_WIDTH), lambda i: (i, 0)),
        compiler_params=_cparams("arbitrary"),
        name="gqa_attention",
    )(proj_b, proj_b, proj_b, proj_b, proj_b, cache_k, cache_v)


OUTPROJ_CHUNKS = 4


def _outproj_kernel(*refs, n_ctx, n_tiles, n_stream):
    x_refs, refs = refs[:n_stream], refs[n_stream:]
    (ya_ref, yb_ref, yc_ref, wo_ref, gate_ref, shift_ref, scale_ref, g2_ref, wr_ref, br_ref,
     xo_ref, h_ref, lg_ref, mix_even, mix_odd) = refs
    i = pl.program_id(0)
    prev_tile = jnp.maximum(i - 1, 0)
    crow = _cond_row(prev_tile, OUTPROJ_TM, n_ctx)
    cols = D_MODEL // OUTPROJ_CHUNKS
    rows = OUTPROJ_TM // OUTPROJ_CHUNKS

    def product(mix_ref, c):
        cs = slice(c * cols, (c + 1) * cols)
        mix_ref[:, cs] = (
            jnp.dot(ya_ref[...], wo_ref[0, 0:POOL_WIDTH, cs], preferred_element_type=F32)
            + jnp.dot(yb_ref[...], wo_ref[0, POOL_WIDTH:POOL_WIDTH + NA_WIDTH, cs], preferred_element_type=F32)
            + jnp.dot(yc_ref[...], wo_ref[0, POOL_WIDTH + NA_WIDTH:, cs], preferred_element_type=F32))

    def finish(mix_ref, c):
        rs = slice(c * rows, (c + 1) * rows)
        x = (_stream_rows(x_refs, prev_tile, OUTPROJ_TM, n_ctx, rs)
             + gate_ref[0, 0, pl.ds(crow, 1), :] * mix_ref[rs, :])
        xo_ref[rs, :] = x
        y = x * lax.rsqrt(jnp.mean(x * x, axis=-1, keepdims=True) + EPS) * g2_ref[...]
        h = y * (1 + scale_ref[0, 0, pl.ds(crow, 1), :]) + shift_ref[0, 0, pl.ds(crow, 1), :]
        h_ref[rs, :] = _pack_bf16_pairs(h)
        lg_ref[rs, :] = jnp.dot(h.astype(BF16), wr_ref[...], preferred_element_type=F32) + br_ref[...]

    @pl.when(i == 0)
    def _():
        for c in range(OUTPROJ_CHUNKS):
            product(mix_even, c)

    for parity, (cur, prev) in enumerate(((mix_even, mix_odd), (mix_odd, mix_even))):
        @pl.when((i > 0) & (i < n_tiles) & (i % 2 == parity))
        def _(cur=cur, prev=prev):
            for c in range(OUTPROJ_CHUNKS):
                product(cur, c)
                finish(prev, c)

    @pl.when(i == n_tiles)
    def _():
        last = mix_even if (n_tiles - 1) % 2 == 0 else mix_odd
        for c in range(OUTPROJ_CHUNKS):
            finish(last, c)


def _outproj(stream, ya, yb, yc, w_out_b, mod, layer, g2, w_router_b, b_router_p, n_ctx):
    n_tok = ya.shape[0]
    tm = OUTPROJ_TM
    n_tiles = n_tok // tm

    def mod_spec(chunk):
        return pl.BlockSpec((1, 1, COND_ROWS, D_MODEL), lambda i: (layer, chunk, 0, 0))

    def cur(width):
        return pl.BlockSpec((tm, width), lambda i: (jnp.minimum(i, n_tiles - 1), 0))

    def prev(width):
        return pl.BlockSpec((tm, width), lambda i: (jnp.maximum(i - 1, 0), 0))

    return pl.pallas_call(
        functools.partial(_outproj_kernel, n_ctx=n_ctx, n_tiles=n_tiles, n_stream=len(stream)),
        out_shape=(jax.ShapeDtypeStruct((n_tok, D_MODEL), F32),
                   jax.ShapeDtypeStruct((n_tok, PACKED_WIDTH), U32),
                   jax.ShapeDtypeStruct((n_tok, LANES), F32)),
        grid=(n_tiles + 1,),
        in_specs=[
            *_stream_specs(stream, tm, lambda i: jnp.maximum(i - 1, 0), n_ctx),
            cur(POOL_WIDTH), cur(NA_WIDTH), cur(GQA_WIDTH),
            pl.BlockSpec((1, D_MODEL, D_MODEL), lambda i: (layer, 0, 0), pipeline_mode=pl.Buffered(1)),
            mod_spec(2), mod_spec(3), mod_spec(4),
            pl.BlockSpec((1, D_MODEL), lambda i: (0, 0)),
            pl.BlockSpec((D_MODEL, LANES), lambda i: (0, 0)),
            pl.BlockSpec((1, LANES), lambda i: (0, 0)),
        ],
        out_specs=(prev(D_MODEL), prev(PACKED_WIDTH), prev(LANES)),
        scratch_shapes=[pltpu.VMEM((tm, D_MODEL), F32), pltpu.VMEM((tm, D_MODEL), F32)],
        compiler_params=_cparams("arbitrary"),
        name="outproj_norm_router",
    )(*stream, ya, yb, yc, w_out_b, mod, mod, mod, g2, w_router_b, b_router_p)


def _route_kernel(lg_ref, code_ref, gate_ref, cnt_ref, carry_scr):
    i = pl.program_id(0)

    @pl.when(i == 0)
    def _():
        carry_scr[...] = jnp.zeros_like(carry_scr)

    tm = lg_ref.shape[0]
    l = lg_ref[...]
    lane = lax.broadcasted_iota(I32, (tm, LANES), 1)
    sel = jnp.zeros((tm, LANES), F32)
    vals, idxs = [], []
    for _ in range(TOP_K):
        m = jnp.max(l, axis=-1, keepdims=True)
        ik = jnp.min(jnp.where(l == m, lane, LANES), axis=-1, keepdims=True)
        hit = lane == ik
        sel = jnp.where(hit, 1.0, sel)
        l = jnp.where(hit, -jnp.inf, l)
        vals.append(m)
        idxs.append(ik)
    exps = [jnp.exp(v - vals[0]) for v in vals]
    tot = exps[0] + exps[1] + exps[2] + exps[3]
    r = lax.broadcasted_iota(I32, (tm, tm), 0)
    c = lax.broadcasted_iota(I32, (tm, tm), 1)
    tri = jnp.where(c < r, 1.0, 0.0).astype(BF16)
    before = jnp.dot(tri, sel.astype(BF16), preferred_element_type=F32) + carry_scr[...]
    code_out = jnp.zeros((tm, LANES), I32)
    gate_out = jnp.zeros((tm, LANES), F32)
    for k in range(TOP_K):
        pk = jnp.sum(jnp.where(lane == idxs[k], before, 0.0), axis=-1, keepdims=True)
        code_out = jnp.where(lane == k, idxs[k] * SLOT_BASE + pk.astype(I32), code_out)
        gate_out = jnp.where(lane == k, exps[k] / tot, gate_out)
    code_ref[...] = code_out
    gate_ref[...] = gate_out
    carry_scr[...] = carry_scr[...] + jnp.sum(sel, axis=0, keepdims=True)
    cnt_ref[...] = jnp.broadcast_to(carry_scr[...], cnt_ref.shape)


def _route(logits):
    n_tok = logits.shape[0]
    assert n_tok <= SLOT_BASE
    tm = ROUTE_TM
    tile = pl.BlockSpec((tm, LANES), lambda i: (i, 0))
    return pl.pallas_call(
        _route_kernel,
        out_shape=(jax.ShapeDtypeStruct((n_tok, LANES), I32),
                   jax.ShapeDtypeStruct((n_tok, LANES), F32),
                   jax.ShapeDtypeStruct((8, LANES), F32)),
        grid=(n_tok // tm,),
        in_specs=[tile],
        out_specs=(tile, tile, pl.BlockSpec((8, LANES), lambda i: (0, 0))),
        scratch_shapes=[pltpu.VMEM((1, LANES), F32)],
        compiler_params=_cparams("arbitrary"),
        name="route_topk",
    )(logits)


ZERO_ROWS = 64


def _dispatch_kernel(cnt_ref, start_ref, nused_ref, row_ref, h_ref, xs_hbm, zero_scr, sem, *, n_blocks):
    i = pl.program_id(0)
    last = pl.num_programs(0) - 1

    def row_copy(src_row, dst_row):
        return pltpu.make_async_copy(h_ref.at[pl.ds(src_row, 1)], xs_hbm.at[pl.ds(dst_row, 1)], sem)

    def zero_row_copy(dst_row):
        return pltpu.make_async_copy(zero_scr.at[pl.ds(0, 1)], xs_hbm.at[pl.ds(dst_row, 1)], sem)

    def zero_chunk_copy(dst_row):
        return pltpu.make_async_copy(zero_scr, xs_hbm.at[pl.ds(dst_row, ZERO_ROWS)], sem)

    def issue(tt, carry):
        t0 = pl.multiple_of(tt * DMA_UNROLL, DMA_UNROLL)
        for u in range(DMA_UNROLL):
            for k in range(TOP_K):
                dst = row_ref[0, 0, (t0 + u) * TOP_K + k]
                pltpu.make_async_copy(h_ref.at[pl.ds(t0, DMA_UNROLL)].at[pl.ds(u, 1)],
                                      xs_hbm.at[pl.ds(dst, 1)], sem).start(priority=k % 2)
        return carry

    lax.fori_loop(0, DISPATCH_TILE // DMA_UNROLL, issue, 0)

    def drain(t, carry):
        for k in range(TOP_K):
            row_copy(0, 0).wait()
        return carry

    lax.fori_loop(0, DISPATCH_TILE, drain, 0, unroll=DMA_UNROLL)

    @pl.when(i == last)
    def _():
        zero_scr[...] = jnp.zeros_like(zero_scr)

        def per_expert(e, carry):
            cnt = cnt_ref[e]
            padded = (cnt + MOE_BLOCK - 1) // MOE_BLOCK * MOE_BLOCK
            base = start_ref[e]

            def z_issue(p, c):
                zero_row_copy(base + p).start()
                return c

            def z_drain(p, c):
                zero_row_copy(0).wait()
                return c

            lax.fori_loop(cnt, padded, z_issue, 0)
            lax.fori_loop(cnt, padded, z_drain, 0)
            return carry

        lax.fori_loop(0, N_EXPERTS, per_expert, 0)

        chunks = MOE_BLOCK // ZERO_ROWS

        def t_issue(c, carry):
            zero_chunk_copy(c * ZERO_ROWS).start()
            return carry

        def t_drain(c, carry):
            zero_chunk_copy(0).wait()
            return carry

        lax.fori_loop(nused_ref[0] * chunks, n_blocks * chunks, t_issue, 0)
        lax.fori_loop(nused_ref[0] * chunks, n_blocks * chunks, t_drain, 0)


def _dispatch(counts, starts, nused, rows, h, n_blocks):
    n_tok = h.shape[0]
    nt = n_tok // DISPATCH_TILE
    grid_spec = pltpu.PrefetchScalarGridSpec(
        num_scalar_prefetch=3,
        grid=(nt,),
        in_specs=[
            pl.BlockSpec((1, 1, DISPATCH_TILE * TOP_K), lambda i, *_: (i, 0, 0), memory_space=pltpu.SMEM),
            pl.BlockSpec((DISPATCH_TILE, PACKED_WIDTH), lambda i, *_: (i, 0)),
        ],
        out_specs=pl.BlockSpec(memory_space=pl.ANY),
        scratch_shapes=[pltpu.VMEM((ZERO_ROWS, PACKED_WIDTH), U32), pltpu.SemaphoreType.DMA],
    )
    return pl.pallas_call(
        functools.partial(_dispatch_kernel, n_blocks=n_blocks),
        out_shape=jax.ShapeDtypeStruct((n_blocks * MOE_BLOCK, PACKED_WIDTH), U32),
        grid_spec=grid_spec,
        compiler_params=_cparams("arbitrary"),
        name="moe_dispatch",
    )(counts, starts, nused, rows.reshape(nt, 1, DISPATCH_TILE * TOP_K), h)


def _expert_kernel(blk_e_ref, next_e_ref, blk_rows_ref, nused_ref, xs_ref, bgu_ref, bd_ref, wgu_hbm, wd_hbm, ys_ref,
                   wgu_in, wd_in, wgu_scr, wd_scr, slot_ref, sems, *, layer):
    b = pl.program_id(0)
    e = blk_e_ref[b]
    prev = blk_e_ref[jnp.maximum(b - 1, 0)]

    def fetch(expert, slot):
        return (pltpu.make_async_copy(wgu_hbm.at[layer, expert], wgu_in.at[slot], sems.at[0, slot]),
                pltpu.make_async_copy(wd_hbm.at[layer, expert], wd_in.at[slot], sems.at[1, slot]))

    @pl.when(b == 0)
    def _():
        slot_ref[0] = 0
        for copy in fetch(e, 0):
            copy.start()

    @pl.when((b == 0) | (e != prev))
    def _():
        slot = slot_ref[0]
        for copy in fetch(e, slot):
            copy.wait()
        wgu_scr[...] = wgu_in[slot].astype(BF16)
        wd_scr[...] = wd_in[slot].astype(BF16)
        nxt = next_e_ref[b]

        @pl.when(nxt != e)
        def _():
            for copy in fetch(nxt, 1 - slot):
                copy.start()

        slot_ref[0] = 1 - slot

    def ffn(rows):
        x_first, x_second = _unpack_bf16_pairs(xs_ref[rows, :])
        gu = (jnp.dot(x_first, wgu_scr[0:PACKED_WIDTH, :], preferred_element_type=F32)
              + jnp.dot(x_second, wgu_scr[PACKED_WIDTH:, :], preferred_element_type=F32) + bgu_ref[0, 0])
        gate = jnp.minimum(gu[:, :D_FF], SWIGLU_LIMIT)
        up = jnp.clip(gu[:, D_FF:], -SWIGLU_LIMIT, SWIGLU_LIMIT)
        act = (up + 1) * (gate * jax.nn.sigmoid(SWIGLU_ALPHA * gate))
        ys_ref[rows, :] = jnp.dot(act.astype(BF16), wd_scr[...], preferred_element_type=F32) + bd_ref[0, 0]

    n_rows = blk_rows_ref[b]
    half = MOE_BLOCK // 2

    @pl.when(n_rows > half)
    def _():
        ffn(slice(None))

    @pl.when((n_rows > 0) & (n_rows <= half))
    def _():
        ffn(slice(0, half))
        ys_ref[half:, :] = jnp.zeros((MOE_BLOCK - half, D_MODEL), F32)

    @pl.when(n_rows == 0)
    def _():
        ys_ref[...] = jnp.zeros_like(ys_ref)


def _experts(blk_e, next_e, blk_rows, nused, xs, w_gate_up, b_gate_up, w_down, b_down, layer):
    n_blocks = blk_e.shape[0]
    depth = w_gate_up.shape[0]
    grid_spec = pltpu.PrefetchScalarGridSpec(
        num_scalar_prefetch=4,
        grid=(n_blocks,),
        in_specs=[
            pl.BlockSpec((MOE_BLOCK, PACKED_WIDTH), lambda b, be, ne, br, nu: (jnp.minimum(b, nu[0] - 1), 0)),
            pl.BlockSpec((1, 1, 1, 2 * D_FF), lambda b, be, ne, br, nu: (layer, be[b], 0, 0)),
            pl.BlockSpec((1, 1, 1, D_MODEL), lambda b, be, ne, br, nu: (layer, be[b], 0, 0)),
            pl.BlockSpec(memory_space=pl.ANY),
            pl.BlockSpec(memory_space=pl.ANY),
        ],
        out_specs=pl.BlockSpec((MOE_BLOCK, D_MODEL), lambda b, be, ne, br, nu: (b, 0)),
        scratch_shapes=[pltpu.VMEM((2, D_MODEL, 2 * D_FF), F32), pltpu.VMEM((2, D_FF, D_MODEL), F32),
                        pltpu.VMEM((D_MODEL, 2 * D_FF), BF16), pltpu.VMEM((D_FF, D_MODEL), BF16),
                        pltpu.SMEM((1,), I32), pltpu.SemaphoreType.DMA((2, 2))],
    )
    return pl.pallas_call(
        functools.partial(_expert_kernel, layer=layer),
        out_shape=jax.ShapeDtypeStruct((xs.shape[0], D_MODEL), F32),
        grid_spec=grid_spec,
        compiler_params=_cparams("arbitrary"),
        name="moe_experts",
    )(blk_e, next_e, blk_rows, nused, xs, b_gate_up.reshape(depth, N_EXPERTS, 1, 2 * D_FF),
      b_down.reshape(depth, N_EXPERTS, 1, D_MODEL), w_gate_up, w_down)


def _combine_kernel(row_ref, next_row_ref, gate_ref, x_ref, mgate_ref, gfin_ref, ys_hbm, *out_and_scratch,
                    n_ctx, final_norm):
    *out_refs, rows_scr, sems = out_and_scratch
    i = pl.program_id(0)
    n_steps = pl.num_programs(0)
    crow = _cond_row(i, TOKEN_TILE, n_ctx)
    slot = i % 2

    def row_copy(buf, src_row, k, t0, u):
        return pltpu.make_async_copy(ys_hbm.at[pl.ds(src_row, 1)],
                                     rows_scr.at[buf, k, pl.ds(t0, DMA_UNROLL)].at[pl.ds(u, 1)],
                                     sems.at[buf])

    def gather(rows, buf):
        def issue(tt, carry):
            t0 = pl.multiple_of(tt * DMA_UNROLL, DMA_UNROLL)
            for u in range(DMA_UNROLL):
                for k in range(TOP_K):
                    row_copy(buf, rows[0, 0, (t0 + u) * TOP_K + k], k, t0, u).start(priority=k % 2)
            return carry

        lax.fori_loop(0, TOKEN_TILE // DMA_UNROLL, issue, 0)

    @pl.when(i == 0)
    def _():
        gather(row_ref, 0)

    @pl.when(i + 1 < n_steps)
    def _():
        gather(next_row_ref, 1 - slot)

    def drain(t, carry):
        for k in range(TOP_K):
            row_copy(slot, 0, 0, 0, 0).wait()
        return carry

    lax.fori_loop(0, TOKEN_TILE, drain, 0, unroll=DMA_UNROLL)

    def consume(o_ref):
        g = gate_ref[...]
        y = g[:, 0:1] * rows_scr[slot, 0]
        for k in range(1, TOP_K):
            y = y + g[:, k:k + 1] * rows_scr[slot, k]
        x = x_ref[...] + mgate_ref[0, 0, pl.ds(crow, 1), :] * y
        if final_norm:
            x = x * lax.rsqrt(jnp.mean(x * x, axis=-1, keepdims=True) + EPS) * gfin_ref[...]
        o_ref[...] = x

    if len(out_refs) == 1:
        consume(out_refs[0])
    else:
        ctx_ref, lat_ref = out_refs
        is_ctx = i * TOKEN_TILE < n_ctx
        pl.when(is_ctx)(lambda: consume(ctx_ref))
        pl.when(jnp.logical_not(is_ctx))(lambda: consume(lat_ref))


def _combine(rows, gates, x, mod, layer, ys, g_final, n_ctx, final_norm):
    n_tok = x.shape[0]
    nt = n_tok // TOKEN_TILE
    ctx_tiles = n_ctx // TOKEN_TILE
    row_block = (1, 1, TOKEN_TILE * TOP_K)
    rows = rows.reshape(nt, 1, TOKEN_TILE * TOP_K)
    tile = (TOKEN_TILE, D_MODEL)
    if final_norm:
        out_shape = (jax.ShapeDtypeStruct((n_ctx, D_MODEL), F32), jax.ShapeDtypeStruct((n_tok - n_ctx, D_MODEL), F32))
        out_specs = (pl.BlockSpec(tile, lambda i: (jnp.minimum(i, ctx_tiles - 1), 0)),
                     pl.BlockSpec(tile, lambda i: (jnp.maximum(i - ctx_tiles, 0), 0)))
    else:
        out_shape = jax.ShapeDtypeStruct((n_tok, D_MODEL), F32)
        out_specs = pl.BlockSpec(tile, lambda i: (i, 0))
    return pl.pallas_call(
        functools.partial(_combine_kernel, n_ctx=n_ctx, final_norm=final_norm),
        out_shape=out_shape,
        grid=(nt,),
        in_specs=[
            pl.BlockSpec(row_block, lambda i: (i, 0, 0), memory_space=pltpu.SMEM),
            pl.BlockSpec(row_block, lambda i: (jnp.minimum(i + 1, nt - 1), 0, 0), memory_space=pltpu.SMEM),
            pl.BlockSpec((TOKEN_TILE, LANES), lambda i: (i, 0)),
            pl.BlockSpec(tile, lambda i: (i, 0)),
            pl.BlockSpec((1, 1, COND_ROWS, D_MODEL), lambda i: (layer, 5, 0, 0)),
            pl.BlockSpec((1, D_MODEL), lambda i: (0, 0)),
            pl.BlockSpec(memory_space=pl.ANY),
        ],
        out_specs=out_specs,
        scratch_shapes=[pltpu.VMEM((2, TOP_K, TOKEN_TILE, D_MODEL), F32), pltpu.SemaphoreType.DMA((2,))],
        compiler_params=_cparams("arbitrary"),
        name="moe_combine",
    )(rows, rows, gates, x, mod, g_final, ys)


def _rope_tables():
    t = np.arange(LAT_SEQ)
    row = (t // GRID_W).astype(np.float32)
    col = (t % GRID_W).astype(np.float32)
    half = HEAD_DIM // 2
    inv = jnp.asarray(ROPE_THETA, F32) ** (-jnp.arange(0, half, 2, dtype=F32) / half)
    ang = jnp.concatenate([row[:, None] * inv, col[:, None] * inv], axis=-1)
    cos, sin = jnp.cos(ang), jnp.sin(ang)
    return jnp.concatenate([cos, cos], axis=-1), jnp.concatenate([-sin, sin], axis=-1)


def _slot_rows(codes, starts):
    expert = lax.shift_right_logical(codes, SLOT_SHIFT)
    base = jnp.sum(jnp.where(expert[..., None] == jnp.arange(N_EXPERTS, dtype=I32), starts, 0), axis=-1)
    return base + (codes & (SLOT_BASE - 1))


def _routing_plan(counts_f, n_blocks):
    counts = counts_f[0, :N_EXPERTS].astype(I32)
    padded = (counts + MOE_BLOCK - 1) // MOE_BLOCK * MOE_BLOCK
    pend = jnp.cumsum(padded)
    starts = pend - padded
    nused = pend[-1] // MOE_BLOCK
    blk = jnp.minimum(jnp.arange(n_blocks, dtype=I32), nused - 1)
    blk_e = jnp.sum((pend[None, :] <= (blk * MOE_BLOCK)[:, None]).astype(I32), axis=1)
    blk_e = jnp.minimum(blk_e, N_EXPERTS - 1)
    ids = jnp.arange(N_EXPERTS, dtype=I32)
    later = (ids[None, :] > blk_e[:, None]) & (counts > 0)[None, :]
    next_e = jnp.min(jnp.where(later, ids[None, :], N_EXPERTS), axis=1)
    next_e = jnp.where(next_e == N_EXPERTS, blk_e, next_e).astype(I32)
    mine = ids[None, :] == blk_e[:, None]
    group_end = jnp.sum(jnp.where(mine, (starts + counts)[None, :], 0), axis=1)
    first_row = jnp.arange(n_blocks, dtype=I32) * MOE_BLOCK
    blk_rows = jnp.where(jnp.arange(n_blocks) < nused, jnp.clip(group_end - first_row, 0, MOE_BLOCK), 0).astype(I32)
    return counts, starts.astype(I32), nused.reshape(1).astype(I32), blk_e, next_e, blk_rows


def _forward(x_prompt, x_sample, cache_na_k, cache_na_v, cache_gqa_k, cache_gqa_v, c, c_ctx,
             w_ada, b_ada, g_norm1, g_norm2, w_in, w_pool, pool_scale, na_rel_bias, g_q, g_k,
             w_out, w_router, b_router, w_gate_up, b_gate_up, w_down, b_down, g_final):
    depth = w_in.shape[0]
    nb_ctx, nb_lat = x_prompt.shape[0], x_sample.shape[0]
    n_ctx, n_lat = nb_ctx * CTX_SEQ, nb_lat * LAT_SEQ
    n_tok = n_ctx + n_lat
    assert x_prompt.shape[1:] == (CTX_SEQ, D_MODEL) and x_sample.shape[1:] == (LAT_SEQ, D_MODEL)
    assert n_ctx % LAT_SEQ == 0 and 1 + nb_lat <= COND_ROWS
    n_blocks = n_tok * TOP_K // MOE_BLOCK + N_EXPERTS

    stream = (x_prompt.reshape(n_ctx, D_MODEL), x_sample.reshape(n_lat, D_MODEL))
    cond = jnp.zeros((COND_ROWS, D_MODEL), F32).at[0].set(c_ctx).at[1:1 + nb_lat].set(c)
    mod = _ada_mod(cond, w_ada, b_ada)

    cos2, sin2 = _rope_tables()
    na_bias = _na_bias_tables(na_rel_bias)
    w_in_b = w_in.astype(BF16)
    w_out_b = w_out.astype(BF16)
    w_pool_b = w_pool.astype(BF16)
    w_router_b = jnp.pad(w_router, ((0, 0), (0, 0), (0, LANES - N_EXPERTS))).astype(BF16)
    b_router_p = jnp.pad(b_router, ((0, 0), (0, LANES - N_EXPERTS)), constant_values=NEG)
    cna_k = cache_na_k.reshape(nb_lat, depth, CTX_SEQ, NA_WIDTH).astype(BF16)
    cna_v = cache_na_v.reshape(nb_lat, depth, CTX_SEQ, NA_WIDTH).astype(BF16)
    cgq_k = cache_gqa_k.reshape(nb_lat, depth, CTX_SEQ, GQA_KV_WIDTH).astype(BF16)
    cgq_v = cache_gqa_v.reshape(nb_lat, depth, CTX_SEQ, GQA_KV_WIDTH).astype(BF16)

    new_kv = []
    for l in range(depth):
        u, kv_f, proj_b = _inproj(stream, mod, l, g_norm1[l][None], w_in_b, g_q[l][None], g_k[l][None],
                                  cos2, sin2, n_ctx)
        new_kv.append(kv_f[:n_ctx])
        ya = _pool(u, w_pool_b[l], pool_scale[l][None], n_ctx)
        yb = _na_attention(proj_b, cna_k, cna_v, na_bias, l, n_ctx)
        yc = _gqa_attention(proj_b, cgq_k, cgq_v, l, n_ctx)
        x, h, logits = _outproj(stream, ya, yb, yc, w_out_b, mod, l, g_norm2[l][None],
                                w_router_b[l], b_router_p[l][None], n_ctx)
        codes, gates, counts_f = _route(logits)
        counts, starts, nused, blk_e, next_e, blk_rows = _routing_plan(counts_f, n_blocks)
        rows = _slot_rows(codes[:, :TOP_K], starts)
        xs = _dispatch(counts, starts, nused, rows, h, n_blocks)
        ys = _experts(blk_e, next_e, blk_rows, nused, xs, w_gate_up, b_gate_up, w_down, b_down, l)
        x = _combine(rows, gates, x, mod, l, ys, g_final[None], n_ctx, final_norm=(l == depth - 1))
        stream = (x,)

    y_ctx, y_lat = x
    y_prompt = y_ctx.reshape(nb_ctx, CTX_SEQ, D_MODEL)
    y_sample = y_lat.reshape(nb_lat, LAT_SEQ, D_MODEL)

    def stack(col, width):
        per_layer = [p[:, col:col + width].reshape(nb_ctx, CTX_SEQ, width // HEAD_DIM, HEAD_DIM) for p in new_kv]
        return jnp.stack(per_layer, axis=1)

    return (y_prompt, y_sample, stack(KVF_KN, NA_WIDTH), stack(KVF_VN, NA_WIDTH),
            stack(KVF_KG, GQA_KV_WIDTH), stack(KVF_VG, GQA_KV_WIDTH))


def kernel(x_prompt, x_sample, cache_na_k, cache_na_v, cache_gqa_k, cache_gqa_v, c, c_ctx, w_ada, b_ada,
           g_norm1, g_norm2, w_in, w_pool, pool_scale, na_rel_bias, g_q, g_k, w_out, w_router, b_router,
           w_gate_up, b_gate_up, w_down, b_down, g_final):
    return _forward(x_prompt, x_sample, cache_na_k, cache_na_v, cache_gqa_k, cache_gqa_v, c, c_ctx,
                    w_ada, b_ada, g_norm1, g_norm2, w_in, w_pool, pool_scale, na_rel_bias, g_q, g_k,
                    w_out, w_router, b_router, w_gate_up, b_gate_up, w_down, b_down, g_final)
```
